```python
import jax, jax.numpy as jnp
from jax import lax
import numpy as np

D_MODEL = 4096
BATCH = 4
SEQ = 4096
DEPTH = 1

MIX_WIDTH = D_MODEL
GLA_WIDTH = MIX_WIDTH // 2
HGRN_WIDTH = MIX_WIDTH - GLA_WIDTH
GLA_HEADS = 4
GLA_KEY_DIM = GLA_WIDTH // 2
GLA_HEAD_K = GLA_KEY_DIM // GLA_HEADS
GLA_HEAD_V = GLA_WIDTH // GLA_HEADS
GLA_RANK = 16
GLA_GATE_NORM = 16.0
HGRN_EXPAND = 128
HGRN_HEADS = HGRN_WIDTH // HGRN_EXPAND
HGRN_HEAD_I = HGRN_WIDTH // HGRN_HEADS
CHUNK = 64
N_EXPERTS = 32
TOP_K = 4
D_FF_EXPERT = D_MODEL // 4
SWIGLU_ALPHA = 1.702
SWIGLU_LIMIT = 7.0
EPS = 1e-5

IN_WIDTHS = (GLA_KEY_DIM, GLA_KEY_DIM, GLA_WIDTH, GLA_WIDTH, GLA_RANK,
             HGRN_WIDTH, HGRN_WIDTH, HGRN_WIDTH, HGRN_WIDTH)
IN_COLS = sum(IN_WIDTHS)
SPLIT_POINTS = tuple(int(c) for c in np.cumsum(IN_WIDTHS)[:-1])

kernel_name = "hybrid_gla_hgrn2_moe_block"


def rms_norm(x, g):
    xf = x.astype(jnp.float32)
    y = xf * lax.rsqrt(jnp.mean(xf * xf, axis=-1, keepdims=True) + EPS)
    return (y * g.astype(jnp.float32)).astype(x.dtype)


def split_heads(t, n_heads):
    b, s, w = t.shape
    return t.reshape(b, s, n_heads, w // n_heads).transpose(0, 2, 1, 3)


def merge_heads(t):
    b, h, s, d = t.shape
    return t.transpose(0, 2, 1, 3).reshape(b, s, h * d)


def chunk_gated_linear_attention(q, k, v, log_decay):
    f32 = jnp.float32
    bsz, n_h, s, dk = q.shape
    dv = v.shape[-1]
    n_chunks = s // CHUNK

    def to_chunks(t):
        return jnp.moveaxis(t.astype(f32).reshape(bsz, n_h, n_chunks, CHUNK, t.shape[-1]), 2, 0)

    qc, kc, vc, gc = (to_chunks(t) for t in (q, k, v, log_decay))
    causal = jnp.tril(jnp.ones((CHUNK, CHUNK), dtype=bool))[:, :, None]

    def step(state, inp):
        qi, ki, vi, gi = inp
        b = jnp.cumsum(gi, axis=2)
        diff = b[:, :, :, None, :] - b[:, :, None, :, :]
        decay = jnp.exp(jnp.where(causal, diff, -jnp.inf))
        scores = jnp.einsum('bhid,bhjd,bhijd->bhij', qi, ki, decay)
        o = (jnp.einsum('bhid,bhdv->bhiv', qi * jnp.exp(b), state)
             + jnp.einsum('bhij,bhjv->bhiv', scores, vi))
        b_last = b[:, :, -1:, :]
        new_state = (jnp.exp(b_last)[:, :, 0, :, None] * state
                     + jnp.einsum('bhjd,bhjv->bhdv', ki * jnp.exp(b_last - b), vi))
        return new_state, o

    init = jnp.zeros((bsz, n_h, dk, dv), f32)
    _, oc = lax.scan(step, init, (qc, kc, vc, gc))
    return jnp.moveaxis(oc, 0, 2).reshape(bsz, n_h, s, dv)


def gated_head_norm(o, gate, w):
    of = o.astype(jnp.float32)
    y = of * lax.rsqrt(jnp.mean(of * of, axis=-1, keepdims=True) + EPS) * w.astype(jnp.float32)
    return y * jax.nn.silu(gate.astype(jnp.float32))


def hybrid_mixer(xn, in_w, gla_up_w, gla_up_b, gla_norm_w, lb, hgrn_norm_w, out_w):
    f32 = jnp.float32
    proj = xn @ in_w
    q_g, k_g, v_g, r_g, lr_g, q_h, f_h, i_h, r_h = jnp.split(proj, SPLIT_POINTS, axis=-1)

    log_alpha = jax.nn.log_sigmoid((lr_g @ gla_up_w + gla_up_b).astype(f32)) / GLA_GATE_NORM
    o_a = chunk_gated_linear_attention(
        split_heads(q_g * (GLA_HEAD_K ** -0.5), GLA_HEADS), split_heads(k_g, GLA_HEADS),
        split_heads(v_g, GLA_HEADS), split_heads(log_alpha, GLA_HEADS))
    o_a = merge_heads(gated_head_norm(o_a, split_heads(r_g, GLA_HEADS), gla_norm_w))

    zf = f_h.astype(f32)
    lbf = lb.astype(f32)
    forget = lbf + (1.0 - lbf) * jax.nn.sigmoid(zf)
    log_f = jnp.log(forget)
    k_h = (1.0 - lbf) * jax.nn.sigmoid(-zf)
    o_b = chunk_gated_linear_attention(
        split_heads(jax.nn.silu(q_h), HGRN_HEADS), split_heads(k_h, HGRN_HEADS),
        split_heads(i_h, HGRN_HEADS), split_heads(log_f, HGRN_HEADS))
    o_b = merge_heads(gated_head_norm(o_b, split_heads(r_h, HGRN_HEADS), hgrn_norm_w))

    o = jnp.concatenate([o_a, o_b], axis=-1).astype(xn.dtype)
    return o @ out_w


def moe_ffn(xn, router_w, router_b, gate_w, gate_b, up_w, up_b, down_w, down_b):
    f32 = jnp.float32
    bsz, s, d = xn.shape
    t = xn.reshape(bsz * s, d)
    logits = (t @ router_w).astype(f32) + router_b.astype(f32)
    top_vals, top_idx = lax.top_k(logits, TOP_K)
    top_w = jax.nn.softmax(top_vals, axis=-1)
    combine = jnp.sum(jax.nn.one_hot(top_idx, N_EXPERTS, dtype=f32) * top_w[..., None], axis=1)
    out = jnp.zeros((bsz * s, d), f32)
    for e in range(N_EXPERTS):
        g = jnp.minimum(t @ gate_w[e] + gate_b[e], SWIGLU_LIMIT)
        u = jnp.clip(t @ up_w[e] + up_b[e], -SWIGLU_LIMIT, SWIGLU_LIMIT)
        h = g * jax.nn.sigmoid(SWIGLU_ALPHA * g) * (u + 1.0)
        y = h @ down_w[e] + down_b[e]
        out = out + combine[:, e:e + 1] * y.astype(f32)
    return out.reshape(bsz, s, d).astype(xn.dtype)


def setup_inputs(seed: int = 0) -> dict:
    key = jax.random.key(seed)
    ks = jax.random.split(key, 20)
    f32 = jnp.float32
    nrm = lambda k, shape, scale: (jax.random.normal(k, shape, f32) * scale)
    return {
        "x": nrm(ks[0], (BATCH, SEQ, D_MODEL), 1.0),
        "attn_norm_w": 1.0 + nrm(ks[1], (DEPTH, D_MODEL), 0.02),
        "in_proj_w": nrm(ks[2], (DEPTH, D_MODEL, IN_COLS), D_MODEL ** -0.5),
        "gla_gate_up_w": nrm(ks[3], (DEPTH, GLA_RANK, GLA_KEY_DIM), GLA_RANK ** -0.5),
        "gla_gate_up_b": nrm(ks[4], (DEPTH, GLA_KEY_DIM), 0.02),
        "gla_out_norm_w": 1.0 + nrm(ks[5], (DEPTH, GLA_HEAD_V), 0.02),
        "hgrn_lb_logits": nrm(ks[6], (DEPTH + 1, HGRN_WIDTH), 0.1),
        "hgrn_out_norm_w": 1.0 + nrm(ks[7], (DEPTH, HGRN_HEAD_I), 0.02),
        "out_proj_w": nrm(ks[8], (DEPTH, MIX_WIDTH, D_MODEL), MIX_WIDTH ** -0.5),
        "ffn_norm_w": 1.0 + nrm(ks[9], (DEPTH, D_MODEL), 0.02),
        "router_w": nrm(ks[10], (DEPTH, D_MODEL, N_EXPERTS), D_MODEL ** -0.5),
        "router_b": nrm(ks[11], (DEPTH, N_EXPERTS), 0.01),
        "expert_gate_w": nrm(ks[12], (DEPTH, N_EXPERTS, D_MODEL, D_FF_EXPERT), D_MODEL ** -0.5),
        "expert_gate_b": nrm(ks[13], (DEPTH, N_EXPERTS, D_FF_EXPERT), 0.02),
        "expert_up_w": nrm(ks[14], (DEPTH, N_EXPERTS, D_MODEL, D_FF_EXPERT), D_MODEL ** -0.5),
        "expert_up_b": nrm(ks[15], (DEPTH, N_EXPERTS, D_FF_EXPERT), 0.02),
        "expert_down_w": nrm(ks[16], (DEPTH, N_EXPERTS, D_FF_EXPERT, D_MODEL), D_FF_EXPERT ** -0.5),
        "expert_down_b": nrm(ks[17], (DEPTH, N_EXPERTS, D_MODEL), 0.02),
        "final_norm_w": 1.0 + nrm(ks[18], (D_MODEL,), 0.02),
    }


def reference(x, attn_norm_w, in_proj_w, gla_gate_up_w, gla_gate_up_b, gla_out_norm_w,
              hgrn_lb_logits, hgrn_out_norm_w, out_proj_w, ffn_norm_w, router_w, router_b,
              expert_gate_w, expert_gate_b, expert_up_w, expert_up_b, expert_down_w,
              expert_down_b, final_norm_w):
    lower_bounds = jnp.cumsum(jax.nn.softmax(hgrn_lb_logits.astype(jnp.float32), axis=0), axis=0)
    h = x
    for l in range(DEPTH):
        xn = rms_norm(h, attn_norm_w[l])
        h = h + hybrid_mixer(xn, in_proj_w[l], gla_gate_up_w[l], gla_gate_up_b[l],
                             gla_out_norm_w[l], lower_bounds[l], hgrn_out_norm_w[l],
                             out_proj_w[l]).astype(h.dtype)
        xn = rms_norm(h, ffn_norm_w[l])
        h = h + moe_ffn(xn, router_w[l], router_b[l], expert_gate_w[l], expert_gate_b[l],
                        expert_up_w[l], expert_up_b[l], expert_down_w[l],
                        expert_down_b[l]).astype(h.dtype)
    return rms_norm(h, final_norm_w)
```

```python
import functools

import jax
import jax.numpy as jnp
from jax import lax
from jax.experimental import pallas as pl
from jax.experimental.pallas import tpu as pltpu

CHUNK = 64
N_EXPERTS = 32
TOP_K = 4
GLA_HEADS = 4
GLA_RANK = 16
GLA_GATE_NORM = 16.0
HGRN_EXPAND = 128
SWIGLU_ALPHA = 1.702
SWIGLU_LIMIT = 7.0
EPS = 1e-5
LANES = 128
VMEM_LIMIT = 56 * 1024 * 1024

F32 = jnp.float32
BF16 = jnp.bfloat16
U32 = jnp.uint32
I32 = jnp.int32


def _params(semantics):
    return pltpu.CompilerParams(dimension_semantics=semantics, vmem_limit_bytes=VMEM_LIMIT)


def _dot(a, b):
    return jnp.dot(a, b, preferred_element_type=F32)


def _dot_nt(a, b):
    return lax.dot_general(a, b, (((1,), (1,)), ((), ())), preferred_element_type=F32)


def _dot_tn(a, b):
    return lax.dot_general(a, b, (((0,), (0,)), ((), ())), preferred_element_type=F32)


def _rmsnorm(x, w):
    return x * lax.rsqrt(jnp.mean(x * x, axis=-1, keepdims=True) + EPS) * w


def _pack_bf16_pairs(x):
    n = x.shape[1] // 2
    lo = pltpu.bitcast(x[:, :n].astype(BF16).astype(F32), U32)
    hi = pltpu.bitcast(x[:, n:].astype(BF16).astype(F32), U32)
    return (lo >> 16) | hi


def _unpack_lo(u):
    return pltpu.bitcast(u << 16, F32)


def _unpack_hi(u):
    return pltpu.bitcast(u & jnp.uint32(0xFFFF0000), F32)


def _norm_gate_kernel(x_ref, nw_ref, wlr_ref, upw_ref, upb_ref, xn_ref, g_ref):
    xn = _rmsnorm(x_ref[...], nw_ref[...]).astype(BF16)
    xn_ref[...] = xn
    lr = _dot(xn, wlr_ref[...])
    z = jnp.dot(lr, upw_ref[...], precision=lax.Precision.HIGHEST,
                preferred_element_type=F32) + upb_ref[...]
    log_sig = jnp.minimum(z, 0.0) - jnp.log1p(jnp.exp(-jnp.abs(z)))
    g_ref[...] = log_sig * (1.0 / GLA_GATE_NORM)


def _norm_gate(x2, nw, wlr, upw, upb, tm):
    t, d = x2.shape
    kd = upw.shape[1]
    return pl.pallas_call(
        _norm_gate_kernel,
        grid=(t // tm,),
        in_specs=[pl.BlockSpec((tm, d), lambda i: (i, 0)),
                  pl.BlockSpec((1, d), lambda i: (0, 0)),
                  pl.BlockSpec((d, LANES), lambda i: (0, 0)),
                  pl.BlockSpec((LANES, kd), lambda i: (0, 0)),
                  pl.BlockSpec((1, kd), lambda i: (0, 0))],
        out_specs=[pl.BlockSpec((tm, d), lambda i: (i, 0)),
                   pl.BlockSpec((tm, kd), lambda i: (i, 0))],
        out_shape=[jax.ShapeDtypeStruct((t, d), BF16), jax.ShapeDtypeStruct((t, kd), F32)],
        compiler_params=_params(("parallel",)),
        name="norm_gate",
    )(x2, nw, wlr, upw, upb)


def _matmul_kernel(a_ref, b_ref, o_ref):
    o_ref[...] = _dot(a_ref[...], b_ref[...]).astype(o_ref.dtype)


def _in_proj(xn, w, tm, tn):
    t, d = xn.shape
    n = w.shape[1]
    return pl.pallas_call(
        _matmul_kernel,
        grid=(n // tn, t // tm),
        in_specs=[pl.BlockSpec((tm, d), lambda j, i: (i, 0)),
                  pl.BlockSpec((d, tn), lambda j, i: (0, j))],
        out_specs=pl.BlockSpec((tm, tn), lambda j, i: (i, j)),
        out_shape=jax.ShapeDtypeStruct((t, n), BF16),
        compiler_params=_params(("parallel", "parallel")),
        name="in_proj",
    )(xn, w)


def _cumsum_rows(g, row):
    c = g.shape[0]
    b = g
    s = 1
    while s < c:
        b = b + jnp.where(row >= s, pltpu.roll(b, s, 0), 0.0)
        s *= 2
    return b


def _score_operands(q, k, b, row):
    c = q.shape[0]
    kb = k.astype(BF16)
    ops = [(q.astype(BF16), kb)]
    p = jnp.where(row >= 1, pltpu.roll(b, 1, 0), 0.0)
    e = b
    s = 1
    while s < c:
        qs = (q * jnp.exp(b - p)).astype(BF16)
        ks = kb if s == 1 else (k * jnp.exp(e - b)).astype(BF16)
        ops.append((qs, ks))
        if 2 * s < c:
            odd = (row & s) != 0
            p = jnp.where(odd, pltpu.roll(p, s, 0), p)
            e = jnp.where(odd, e, pltpu.roll(e, c - s, 0))
        s *= 2
    return ops


def _score_masks(c):
    row = lax.broadcasted_iota(I32, (c, c), 0)
    col = lax.broadcasted_iota(I32, (c, c), 1)
    masks = [row == col]
    s, sh = 1, 0
    while s < c:
        masks.append((((row ^ col) >> sh) == 1) & ((row & s) != 0))
        s *= 2
        sh += 1
    return masks


def _recurrence_chunk(q, k, g, v, r, nw, st_ref, o_ref, rows, n_heads, dk, dv, masks):
    c = q.shape[0]
    row = lax.broadcasted_iota(I32, (c, 1), 0)
    b = _cumsum_rows(g, row)
    ops = _score_operands(q, k, b, row)
    qd = (q * jnp.exp(b)).astype(BF16)
    b_last = b[c - 1:c, :]
    kd = (k * jnp.exp(b_last - b)).astype(BF16)
    dec = jnp.exp(b_last)
    for h in range(n_heads):
        sk = slice(h * dk, (h + 1) * dk)
        sv = slice(h * dv, (h + 1) * dv)
        a = None
        for (qs, ks), m in zip(ops, masks):
            part = jnp.where(m, _dot_nt(qs[:, sk], ks[:, sk]), 0.0)
            a = part if a is None else a + part
        st = st_ref[h]
        vh = v[:, sv]
        o = _dot_nt(qd[:, sk], st.astype(BF16)) + _dot(a.astype(BF16), vh)
        st_ref[h] = st * dec[:, sk] + _dot_tn(vh, kd[:, sk])
        y = o * lax.rsqrt(jnp.mean(o * o, axis=-1, keepdims=True) + EPS) * nw
        rh = r[:, sv]
        o_ref[rows, sv] = (y * (rh * jax.nn.sigmoid(rh))).astype(o_ref.dtype)


def _gla_kernel(q_ref, k_ref, v_ref, r_ref, g_ref, nw_ref, o_ref, st_ref, *, scale, n_chunks):
    @pl.when(pl.program_id(2) == 0)
    def _():
        st_ref[...] = jnp.zeros_like(st_ref)

    masks = _score_masks(CHUNK)
    dk = q_ref.shape[1]
    dv = v_ref.shape[1]
    nw = nw_ref[...]

    def body(ci, carry):
        rows = pl.ds(pl.multiple_of(ci * CHUNK, CHUNK), CHUNK)
        q = q_ref[rows, :].astype(F32) * scale
        k = k_ref[rows, :].astype(F32)
        _recurrence_chunk(q, k, g_ref[rows, :], v_ref[rows, :], r_ref[rows, :].astype(F32), nw,
                          st_ref, o_ref, rows, 1, dk, dv, masks)
        return carry

    lax.fori_loop(0, n_chunks, body, 0)


def _gla(proj, g, nw, batch, seq, ts):
    t, _ = proj.shape
    kd = g.shape[1]
    hk = kd // GLA_HEADS
    hv = 2 * hk
    ns = seq // ts
    kb = kd // hk
    vb = 2 * kd // hv
    rb = vb + GLA_HEADS
    row = lambda b, h, s: b * ns + s
    return pl.pallas_call(
        functools.partial(_gla_kernel, scale=hk ** -0.5, n_chunks=ts // CHUNK),
        grid=(batch, GLA_HEADS, ns),
        in_specs=[pl.BlockSpec((ts, hk), lambda b, h, s: (row(b, h, s), h)),
                  pl.BlockSpec((ts, hk), lambda b, h, s: (row(b, h, s), kb + h)),
                  pl.BlockSpec((ts, hv), lambda b, h, s: (row(b, h, s), vb + h)),
                  pl.BlockSpec((ts, hv), lambda b, h, s: (row(b, h, s), rb + h)),
                  pl.BlockSpec((ts, hk), lambda b, h, s: (row(b, h, s), h)),
                  pl.BlockSpec((1, hv), lambda b, h, s: (0, 0))],
        out_specs=pl.BlockSpec((ts, hv), lambda b, h, s: (row(b, h, s), h)),
        out_shape=jax.ShapeDtypeStruct((t, GLA_HEADS * hv), BF16),
        scratch_shapes=[pltpu.VMEM((1, hv, hk), F32)],
        compiler_params=_params(("parallel", "parallel", "arbitrary")),
        name="gla",
    )(proj, proj, proj, proj, g, nw)


def _hgrn_kernel(q_ref, f_ref, i_ref, r_ref, lb_ref, nw_ref, o_ref, st_ref, *, n_chunks):
    @pl.when(pl.program_id(1) == 0)
    def _():
        st_ref[...] = jnp.zeros_like(st_ref)

    masks = _score_masks(CHUNK)
    n_heads = st_ref.shape[0]
    lb = lb_ref[...]
    nw = nw_ref[...]

    def body(ci, carry):
        rows = pl.ds(pl.multiple_of(ci * CHUNK, CHUNK), CHUNK)
        zf = f_ref[rows, :].astype(F32)
        g = jnp.log(lb + (1.0 - lb) * jax.nn.sigmoid(zf))
        k = (1.0 - lb) * jax.nn.sigmoid(-zf)
        qz = q_ref[rows, :].astype(F32)
        q = qz * jax.nn.sigmoid(qz)
        _recurrence_chunk(q, k, g, i_ref[rows, :], r_ref[rows, :].astype(F32), nw,
                          st_ref, o_ref, rows, n_heads, HGRN_EXPAND, HGRN_EXPAND, masks)
        return carry

    lax.fori_loop(0, n_chunks, body, 0)


def _hgrn(proj, lb, nw, batch, seq, ts, hw, col0):
    t, _ = proj.shape
    ns = seq // ts
    n_heads = hw // HGRN_EXPAND
    row = lambda b, s: b * ns + s
    return pl.pallas_call(
        functools.partial(_hgrn_kernel, n_chunks=ts // CHUNK),
        grid=(batch, ns),
        in_specs=[pl.BlockSpec((ts, hw), lambda b, s: (row(b, s), col0)),
                  pl.BlockSpec((ts, hw), lambda b, s: (row(b, s), col0 + 1)),
                  pl.BlockSpec((ts, hw), lambda b, s: (row(b, s), col0 + 2)),
                  pl.BlockSpec((ts, hw), lambda b, s: (row(b, s), col0 + 3)),
                  pl.BlockSpec((1, hw), lambda b, s: (0, 0)),
                  pl.BlockSpec((1, HGRN_EXPAND), lambda b, s: (0, 0))],
        out_specs=pl.BlockSpec((ts, hw), lambda b, s: (row(b, s), 0)),
        out_shape=jax.ShapeDtypeStruct((t, hw), BF16),
        scratch_shapes=[pltpu.VMEM((n_heads, HGRN_EXPAND, HGRN_EXPAND), F32)],
        compiler_params=_params(("parallel", "arbitrary")),
        name="hgrn",
    )(proj, proj, proj, proj, lb, nw)


def _out_proj_kernel(oa_ref, ob_ref, wa_ref, wb_ref, x_ref, h_ref):
    h_ref[...] = x_ref[...] + _dot(oa_ref[...], wa_ref[...]) + _dot(ob_ref[...], wb_ref[...])


def _out_proj(oa, ob, wa, wb, x2, tm, tn):
    t, d = x2.shape
    ka, kb = oa.shape[1], ob.shape[1]
    return pl.pallas_call(
        _out_proj_kernel,
        grid=(d // tn, t // tm),
        in_specs=[pl.BlockSpec((tm, ka), lambda j, i: (i, 0)),
                  pl.BlockSpec((tm, kb), lambda j, i: (i, 0)),
                  pl.BlockSpec((ka, tn), lambda j, i: (0, j)),
                  pl.BlockSpec((kb, tn), lambda j, i: (0, j)),
                  pl.BlockSpec((tm, tn), lambda j, i: (i, j))],
        out_specs=pl.BlockSpec((tm, tn), lambda j, i: (i, j)),
        out_shape=jax.ShapeDtypeStruct((t, d), F32),
        compiler_params=_params(("parallel", "parallel")),
        name="out_proj",
    )(oa, ob, wa, wb, x2)


def _router_kernel(h_ref, nw_ref, rw_ref, rb_ref, topi_ref, topw_ref, rank_ref, cnt_ref, run_ref):
    step = pl.program_id(0)

    @pl.when(step == 0)
    def _():
        run_ref[...] = jnp.zeros_like(run_ref)

    tr = h_ref.shape[0]
    xn = _rmsnorm(h_ref[...], nw_ref[...])
    logits = jnp.dot(xn, rw_ref[...], precision=lax.Precision.HIGHEST,
                     preferred_element_type=F32) + rb_ref[...]
    lane = lax.broadcasted_iota(I32, (tr, LANES), 1)
    neg = jnp.float32(-jnp.inf)
    cur = jnp.where(lane < N_EXPERTS, logits, neg)
    vals, sels = [], []
    topi = jnp.zeros((tr, LANES), I32)
    for kk in range(TOP_K):
        m = jnp.max(cur, axis=-1, keepdims=True)
        idx = jnp.min(jnp.where(cur == m, lane, LANES), axis=-1, keepdims=True)
        sel = lane == idx
        vals.append(m)
        sels.append(sel)
        topi = jnp.where(lane == kk, idx, topi)
        cur = jnp.where(sel, neg, cur)
    exps = [jnp.exp(v - vals[0]) for v in vals]
    denom = exps[0] + exps[1] + exps[2] + exps[3]
    topw = jnp.zeros((tr, LANES), F32)
    hot = jnp.zeros((tr, LANES), F32)
    for kk in range(TOP_K):
        topw = jnp.where(lane == kk, exps[kk] / denom, topw)
        hot = hot + sels[kk].astype(F32)
    r_i = lax.broadcasted_iota(I32, (tr, tr), 0)
    c_i = lax.broadcasted_iota(I32, (tr, tr), 1)
    lower = (c_i < r_i).astype(BF16)
    before = _dot(lower, hot.astype(BF16)) + run_ref[...]
    rank = jnp.zeros((tr, LANES), F32)
    for kk in range(TOP_K):
        rk = jnp.sum(jnp.where(sels[kk], before, 0.0), axis=-1, keepdims=True)
        rank = jnp.where(lane == kk, rk, rank)
    topi_ref[...] = topi
    topw_ref[...] = topw
    rank_ref[...] = rank.astype(I32)
    run_ref[...] = run_ref[...] + jnp.sum(hot, axis=0, keepdims=True)
    cnt_ref[...] = run_ref[...]


def _router(h1, nw, rw, rb, tr):
    t, d = h1.shape
    tile = lambda i: (i, 0)
    fixed = lambda i: (0, 0)
    return pl.pallas_call(
        _router_kernel,
        grid=(t // tr,),
        in_specs=[pl.BlockSpec((tr, d), tile),
                  pl.BlockSpec((1, d), fixed),
                  pl.BlockSpec((d, LANES), fixed),
                  pl.BlockSpec((1, LANES), fixed)],
        out_specs=[pl.BlockSpec((tr, LANES), tile),
                   pl.BlockSpec((tr, LANES), tile),
                   pl.BlockSpec((tr, LANES), tile),
                   pl.BlockSpec((1, LANES), fixed)],
        out_shape=[jax.ShapeDtypeStruct((t, LANES), I32),
                   jax.ShapeDtypeStruct((t, LANES), F32),
                   jax.ShapeDtypeStruct((t, LANES), I32),
                   jax.ShapeDtypeStruct((1, LANES), F32)],
        scratch_shapes=[pltpu.VMEM((1, LANES), F32)],
        compiler_params=_params(("arbitrary",)),
        name="router",
    )(h1, nw, rw, rb)


def _dispatch_kernel(pos_ref, h_ref, nw_ref, xs_in_ref, xs_ref, buf_ref, sem):
    del xs_in_ref
    td = h_ref.shape[0]
    buf_ref[...] = _pack_bf16_pairs(_rmsnorm(h_ref[...], nw_ref[...]))

    def issue(ti, carry):
        for kk in range(TOP_K):
            p = pos_ref[0, 0, kk * td + ti]
            pltpu.make_async_copy(buf_ref.at[pl.ds(ti, 1)], xs_ref.at[pl.ds(p, 1)], sem).start()
        return carry

    lax.fori_loop(0, td, issue, 0)
    for kk in range(TOP_K):
        pltpu.make_async_copy(buf_ref, xs_ref.at[pl.ds(0, td)], sem).wait()


def _dispatch(pos_tiles, h1, nw, xs_zero, td):
    t, d = h1.shape
    return pl.pallas_call(
        _dispatch_kernel,
        grid=(t // td,),
        in_specs=[pl.BlockSpec((1, 1, TOP_K * td), lambda i: (i, 0, 0), memory_space=pltpu.SMEM),
                  pl.BlockSpec((td, d), lambda i: (i, 0)),
                  pl.BlockSpec((1, d), lambda i: (0, 0)),
                  pl.BlockSpec(memory_space=pl.ANY)],
        out_specs=pl.BlockSpec(memory_space=pl.ANY),
        out_shape=jax.ShapeDtypeStruct(xs_zero.shape, U32),
        scratch_shapes=[pltpu.VMEM((td, d // 2), U32), pltpu.SemaphoreType.DMA],
        input_output_aliases={3: 0},
        compiler_params=_params(("arbitrary",)),
        name="dispatch",
    )(pos_tiles, h1, nw, xs_zero)


def _experts_kernel(te_ref, tb_ref, nu_ref, x_ref, gw_ref, gb_ref, uw_ref, ub_ref, dw_ref, db_ref,
                    y_ref, xb_ref, acc_ref):
    del te_ref, tb_ref
    i = pl.program_id(0)
    f = pl.program_id(1)
    nf = pl.num_programs(1)
    half = x_ref.shape[1]

    @pl.when(i < nu_ref[0])
    def _():
        @pl.when(f == 0)
        def _():
            u = x_ref[...]
            xb_ref[:, :half] = _unpack_lo(u).astype(BF16)
            xb_ref[:, half:] = _unpack_hi(u).astype(BF16)

        xb = xb_ref[...]
        g = jnp.minimum(_dot(xb, gw_ref[0].astype(BF16)) + gb_ref[0], SWIGLU_LIMIT)
        u = jnp.clip(_dot(xb, uw_ref[0].astype(BF16)) + ub_ref[0], -SWIGLU_LIMIT, SWIGLU_LIMIT)
        hid = g * jax.nn.sigmoid(SWIGLU_ALPHA * g) * (u + 1.0)
        part = _dot(hid.astype(BF16), dw_ref[0].astype(BF16))

        @pl.when(f == 0)
        def _():
            acc_ref[...] = part

        @pl.when(f > 0)
        def _():
            acc_ref[...] += part

        @pl.when(f == nf - 1)
        def _():
            y_ref[...] = _pack_bf16_pairs(acc_ref[...] + db_ref[0])

    @pl.when((i >= nu_ref[0]) & (f == 0))
    def _():
        y_ref[...] = jnp.zeros_like(y_ref)


def _experts(tile_expert, tile_block, n_used, xs, gw, gb, uw, ub, dw, db, tm, tf):
    p, half = xs.shape
    d = 2 * half
    ff = gw.shape[2]
    nf = ff // tf
    n_tiles = p // tm

    def fidx(i, f, nu):
        return jnp.where(i < nu[0], f, nf - 1)

    grid_spec = pltpu.PrefetchScalarGridSpec(
        num_scalar_prefetch=3,
        grid=(n_tiles, nf),
        in_specs=[pl.BlockSpec((tm, half), lambda i, f, te, tb, nu: (tb[i], 0)),
                  pl.BlockSpec((1, d, tf), lambda i, f, te, tb, nu: (te[i], 0, fidx(i, f, nu))),
                  pl.BlockSpec((1, 1, tf), lambda i, f, te, tb, nu: (te[i], 0, fidx(i, f, nu))),
                  pl.BlockSpec((1, d, tf), lambda i, f, te, tb, nu: (te[i], 0, fidx(i, f, nu))),
                  pl.BlockSpec((1, 1, tf), lambda i, f, te, tb, nu: (te[i], 0, fidx(i, f, nu))),
                  pl.BlockSpec((1, tf, d), lambda i, f, te, tb, nu: (te[i], fidx(i, f, nu), 0)),
                  pl.BlockSpec((1, 1, d), lambda i, f, te, tb, nu: (te[i], 0, 0))],
        out_specs=pl.BlockSpec((tm, half), lambda i, f, te, tb, nu: (i, 0)),
        scratch_shapes=[pltpu.VMEM((tm, d), BF16), pltpu.VMEM((tm, d), F32)],
    )
    return pl.pallas_call(
        _experts_kernel,
        grid_spec=grid_spec,
        out_shape=jax.ShapeDtypeStruct((p, half), U32),
        compiler_params=_params(("arbitrary", "arbitrary")),
        name="experts",
    )(tile_expert, tile_block, n_used, xs, gw, gb, uw, ub, dw, db)


def _combine_kernel(pos_ref, h_ref, w_ref, nw_ref, y_ref, o_ref, buf_ref, sem):
    tc = h_ref.shape[0]
    half = buf_ref.shape[2]

    def issue(ti, carry):
        for kk in range(TOP_K):
            p = pos_ref[0, 0, kk * tc + ti]
            pltpu.make_async_copy(y_ref.at[pl.ds(p, 1)], buf_ref.at[kk, pl.ds(ti, 1)], sem).start()
        return carry

    lax.fori_loop(0, tc, issue, 0)
    for kk in range(TOP_K):
        pltpu.make_async_copy(y_ref.at[pl.ds(0, tc)], buf_ref.at[kk], sem).wait()

    w = w_ref[...]
    h = h_ref[...]
    lo = h[:, :half]
    hi = h[:, half:]
    for kk in range(TOP_K):
        u = buf_ref[kk]
        wk = w[:, kk:kk + 1]
        lo = lo + wk * _unpack_lo(u)
        hi = hi + wk * _unpack_hi(u)
    ms = (jnp.sum(lo * lo, axis=-1, keepdims=True) + jnp.sum(hi * hi, axis=-1, keepdims=True)) / (2 * half)
    inv = lax.rsqrt(ms + EPS)
    nw = nw_ref[...]
    o_ref[:, :half] = lo * inv * nw[:, :half]
    o_ref[:, half:] = hi * inv * nw[:, half:]


def _combine(pos_tiles, h1, topw, nw, y, tc):
    t, d = h1.shape
    return pl.pallas_call(
        _combine_kernel,
        grid=(t // tc,),
        in_specs=[pl.BlockSpec((1, 1, TOP_K * tc), lambda i: (i, 0, 0), memory_space=pltpu.SMEM),
                  pl.BlockSpec((tc, d), lambda i: (i, 0)),
                  pl.BlockSpec((tc, LANES), lambda i: (i, 0)),
                  pl.BlockSpec((1, d), lambda i: (0, 0)),
                  pl.BlockSpec(memory_space=pl.ANY)],
        out_specs=pl.BlockSpec((tc, d), lambda i: (i, 0)),
        out_shape=jax.ShapeDtypeStruct((t, d), F32),
        scratch_shapes=[pltpu.VMEM((TOP_K, tc, d // 2), U32), pltpu.SemaphoreType.DMA],
        compiler_params=_params(("arbitrary",)),
        name="combine",
    )(pos_tiles, h1, topw, nw, y)


def _tiles(t, seq, d):
    return dict(
        norm_tm=min(512, t),
        proj_tm=min(1024, t), proj_tn=min(1024, d // 2),
        rec_ts=min(512, seq),
        out_tm=min(512, t), out_tn=min(1024, d),
        router_tr=min(512, t),
        scatter_td=min(256, t),
        expert_tm=min(512, t), expert_tf=min(128, d // 4),
        combine_tc=min(256, t),
    )


def _pos_tiles(pos, tile):
    t = pos.shape[0]
    return pos.reshape(t // tile, tile, TOP_K).transpose(0, 2, 1).reshape(t // tile, 1, TOP_K * tile)


def kernel(x, attn_norm_w, in_proj_w, gla_gate_up_w, gla_gate_up_b, gla_out_norm_w, hgrn_lb_logits, hgrn_out_norm_w, out_proj_w, ffn_norm_w, router_w, router_b, expert_gate_w, expert_gate_b, expert_up_w, expert_up_b, expert_down_w, expert_down_b, final_norm_w):
    batch, seq, d = x.shape
    depth = attn_norm_w.shape[0]
    assert depth == 1 and d % 2048 == 0 and seq % CHUNK == 0
    t = batch * seq
    kd = d // 4
    gw = d // 2
    hw = d - gw
    ff = expert_gate_w.shape[-1]
    ts = _tiles(t, seq, d)

    lower = jnp.cumsum(jax.nn.softmax(hgrn_lb_logits.astype(F32), axis=0), axis=0)[0:1]

    w_in = in_proj_w[0]
    c_lr = 2 * kd + 2 * gw
    w_main = jnp.concatenate([w_in[:, :c_lr], w_in[:, c_lr + GLA_RANK:]], axis=1).astype(BF16)
    w_lr = jnp.pad(w_in[:, c_lr:c_lr + GLA_RANK], ((0, 0), (0, LANES - GLA_RANK))).astype(BF16)
    up_w = jnp.pad(gla_gate_up_w[0], ((0, LANES - GLA_RANK), (0, 0)))
    w_out = out_proj_w[0].astype(BF16)
    r_w = jnp.pad(router_w[0], ((0, 0), (0, LANES - N_EXPERTS)))
    r_b = jnp.pad(router_b, ((0, 0), (0, LANES - N_EXPERTS)))

    x2 = x.reshape(t, d)
    xn, g_gla = _norm_gate(x2, attn_norm_w, w_lr, up_w, gla_gate_up_b, ts["norm_tm"])
    proj = _in_proj(xn, w_main, ts["proj_tm"], ts["proj_tn"])
    o_a = _gla(proj, g_gla, gla_out_norm_w, batch, seq, ts["rec_ts"])
    o_b = _hgrn(proj, lower, hgrn_out_norm_w, batch, seq, ts["rec_ts"], hw, c_lr // hw)
    h1 = _out_proj(o_a, o_b, w_out[:gw], w_out[gw:], x2, ts["out_tm"], ts["out_tn"])

    topi, topw, rank, counts = _router(h1, ffn_norm_w, r_w, r_b, ts["router_tr"])

    tm = ts["expert_tm"]
    n_tiles = (t * TOP_K) // tm + N_EXPERTS
    cnt = counts[0, :N_EXPERTS].astype(I32)
    tiles_e = (cnt + tm - 1) // tm
    tile_end = jnp.cumsum(tiles_e)
    row_start = (tile_end - tiles_e) * tm
    n_used = tile_end[-1]
    pos = row_start[topi[:, :TOP_K]] + rank[:, :TOP_K]
    tile_id = jnp.arange(n_tiles, dtype=I32)
    tile_block = jnp.minimum(tile_id, n_used - 1)
    tile_expert = jnp.sum((tile_block[:, None] >= tile_end[None, :]).astype(I32), axis=1)

    xs_zero = jnp.zeros((n_tiles * tm, d // 2), U32)
    xs = _dispatch(_pos_tiles(pos, ts["scatter_td"]), h1, ffn_norm_w, xs_zero, ts["scatter_td"])
    y = _experts(tile_expert, tile_block, n_used.reshape(1), xs,
                 expert_gate_w[0], expert_gate_b[0][:, None, :], expert_up_w[0], expert_up_b[0][:, None, :],
                 expert_down_w[0], expert_down_b[0][:, None, :], tm, ts["expert_tf"])
    out = _combine(_pos_tiles(pos, ts["combine_tc"]), h1, topw, final_norm_w[None, :], y, ts["combine_tc"])
    return out.reshape(batch, seq, d)
```

```python
import functools

import jax
import jax.numpy as jnp
from jax import lax
from jax.experimental import pallas as pl
from jax.experimental.pallas import tpu as pltpu

CHUNK = 64
N_EXPERTS = 32
TOP_K = 4
GLA_HEADS = 4
GLA_RANK = 16
GLA_GATE_NORM = 16.0
HGRN_EXPAND = 128
SWIGLU_ALPHA = 1.702
SWIGLU_LIMIT = 7.0
EPS = 1e-5
LANES = 128
VMEM_LIMIT = 56 * 1024 * 1024

F32 = jnp.float32
BF16 = jnp.bfloat16
U32 = jnp.uint32
I32 = jnp.int32


def _params(semantics):
    return pltpu.CompilerParams(dimension_semantics=semantics, vmem_limit_bytes=VMEM_LIMIT)


def _dot(a, b):
    return jnp.dot(a, b, preferred_element_type=F32)


def _dot_nt(a, b):
    return lax.dot_general(a, b, (((1,), (1,)), ((), ())), preferred_element_type=F32)


def _dot_tn(a, b):
    return lax.dot_general(a, b, (((0,), (0,)), ((), ())), preferred_element_type=F32)


def _rmsnorm(x, w):
    return x * lax.rsqrt(jnp.mean(x * x, axis=-1, keepdims=True) + EPS) * w


def _pack_bf16_pairs(x):
    n = x.shape[1] // 2
    return _pack2(x[:, :n], x[:, n:])


def _pack2(lo, hi):
    return pltpu.pack_elementwise([lo, hi], packed_dtype=BF16)


def _unpack_lo(u):
    return pltpu.unpack_elementwise(u, index=0, packed_dtype=BF16, unpacked_dtype=F32)


def _unpack_hi(u):
    return pltpu.unpack_elementwise(u, index=1, packed_dtype=BF16, unpacked_dtype=F32)


def _norm_gate_kernel(x_ref, nw_ref, wlr_ref, upw_ref, upb_ref, xn_ref, g_ref):
    xn = _rmsnorm(x_ref[...], nw_ref[...]).astype(BF16)
    xn_ref[...] = xn
    lr = _dot(xn, wlr_ref[...])
    z = jnp.dot(lr, upw_ref[...], precision=lax.Precision.HIGHEST,
                preferred_element_type=F32) + upb_ref[...]
    log_sig = jnp.minimum(z, 0.0) - jnp.log1p(jnp.exp(-jnp.abs(z)))
    g_ref[...] = log_sig * (1.0 / GLA_GATE_NORM)


def _norm_gate(x2, nw, wlr, upw, upb, tm):
    t, d = x2.shape
    kd = upw.shape[1]
    return pl.pallas_call(
        _norm_gate_kernel,
        grid=(t // tm,),
        in_specs=[pl.BlockSpec((tm, d), lambda i: (i, 0)),
                  pl.BlockSpec((1, d), lambda i: (0, 0)),
                  pl.BlockSpec((d, LANES), lambda i: (0, 0)),
                  pl.BlockSpec((LANES, kd), lambda i: (0, 0)),
                  pl.BlockSpec((1, kd), lambda i: (0, 0))],
        out_specs=[pl.BlockSpec((tm, d), lambda i: (i, 0)),
                   pl.BlockSpec((tm, kd), lambda i: (i, 0))],
        out_shape=[jax.ShapeDtypeStruct((t, d), BF16), jax.ShapeDtypeStruct((t, kd), F32)],
        compiler_params=_params(("parallel",)),
        name="norm_gate",
    )(x2, nw, wlr, upw, upb)


def _matmul_kernel(a_ref, b_ref, o_ref):
    o_ref[...] = _dot(a_ref[...], b_ref[...]).astype(o_ref.dtype)


def _in_proj(xn, w, tm, tn):
    t, d = xn.shape
    n = w.shape[1]
    return pl.pallas_call(
        _matmul_kernel,
        grid=(n // tn, t // tm),
        in_specs=[pl.BlockSpec((tm, d), lambda j, i: (i, 0)),
                  pl.BlockSpec((d, tn), lambda j, i: (0, j))],
        out_specs=pl.BlockSpec((tm, tn), lambda j, i: (i, j)),
        out_shape=jax.ShapeDtypeStruct((t, n), BF16),
        compiler_params=_params(("parallel", "parallel")),
        name="in_proj",
    )(xn, w)


def _decay_levels(w):
    c = w.shape[0]
    row = lax.broadcasted_iota(I32, (c, 1), 0)
    pre, suf, tot = w, None, w
    levels = []
    s = 1
    while s < 8:
        levels.append((pre, suf))
        odd = (row & s) != 0
        prev_tot = pltpu.roll(tot, s, 0)
        next_tot = pltpu.roll(tot, c - s, 0)
        pre = pre * jnp.where(odd, prev_tot, 1.0)
        grow = jnp.where(odd, 1.0, next_tot)
        suf = grow if suf is None else suf * grow
        tot = tot * jnp.where(odd, prev_tot, next_tot)
        s *= 2
    groups = c // 8
    pre_g = [pre[8 * r:8 * r + 8] for r in range(groups)]
    suf_g = [suf[8 * r:8 * r + 8] for r in range(groups)]
    tot_g = [tot[8 * r:8 * r + 8] for r in range(groups)]
    m = 1
    while 8 * m < c:
        levels.append((jnp.concatenate(pre_g, axis=0), jnp.concatenate(suf_g, axis=0)))
        new_tot = []
        for r in range(groups):
            if (r // m) & 1:
                pre_g[r] = pre_g[r] * tot_g[r - m]
                new_tot.append(tot_g[r] * tot_g[r - m])
            else:
                suf_g[r] = suf_g[r] * tot_g[r + m]
                new_tot.append(tot_g[r] * tot_g[r + m])
        tot_g = new_tot
        m *= 2
    return levels, jnp.concatenate(pre_g, axis=0), jnp.concatenate(suf_g, axis=0), tot_g[0][0:1]


def _score_masks(c):
    row = lax.broadcasted_iota(I32, (c, c), 0)
    col = lax.broadcasted_iota(I32, (c, c), 1)
    masks = [row == col]
    s, sh = 1, 0
    while s < c:
        masks.append((((row ^ col) >> sh) == 1) & ((row & s) != 0))
        s *= 2
        sh += 1
    return masks


def _recurrence_chunk(q, k, w, v, r, nw, st_ref, o_ref, rows, n_heads, dk, dv, masks):
    levels, pre_c, suf_c, dec = _decay_levels(w)
    kb = k.astype(BF16)
    ops = [(q.astype(BF16), kb)]
    for pre, suf in levels:
        ops.append(((q * pre).astype(BF16), kb if suf is None else (k * suf).astype(BF16)))
    qd = (q * pre_c).astype(BF16)
    kd = (k * suf_c).astype(BF16)
    for h in range(n_heads):
        sk = slice(h * dk, (h + 1) * dk)
        sv = slice(h * dv, (h + 1) * dv)
        a = None
        for (qs, ks), m in zip(ops, masks):
            part = jnp.where(m, _dot_nt(qs[:, sk], ks[:, sk]), 0.0)
            a = part if a is None else a + part
        st = st_ref[h]
        vh = v[:, sv]
        o = _dot_nt(qd[:, sk], st.astype(BF16)) + _dot(a.astype(BF16), vh)
        st_ref[h] = st * dec[:, sk] + _dot_tn(vh, kd[:, sk])
        y = o * lax.rsqrt(jnp.mean(o * o, axis=-1, keepdims=True) + EPS) * nw
        rh = r[:, sv]
        o_ref[rows, sv] = (y * (rh * jax.nn.sigmoid(rh))).astype(o_ref.dtype)


def _gla_kernel(q_ref, k_ref, v_ref, r_ref, g_ref, nw_ref, o_ref, st_ref, *, scale, n_chunks):
    @pl.when(pl.program_id(1) == 0)
    def _():
        st_ref[...] = jnp.zeros_like(st_ref)

    masks = _score_masks(CHUNK)
    n_heads, dv, dk = st_ref.shape
    nw = nw_ref[...]

    def body(ci, carry):
        rows = pl.ds(pl.multiple_of(ci * CHUNK, CHUNK), CHUNK)
        q = q_ref[rows, :].astype(F32) * scale
        k = k_ref[rows, :].astype(F32)
        _recurrence_chunk(q, k, jnp.exp(g_ref[rows, :]), v_ref[rows, :], r_ref[rows, :].astype(F32), nw,
                          st_ref, o_ref, rows, n_heads, dk, dv, masks)
        return carry

    lax.fori_loop(0, n_chunks, body, 0, unroll=2)


def _gla(proj, g, nw, batch, seq, ts):
    t, _ = proj.shape
    kd = g.shape[1]
    gw = 2 * kd
    hk = kd // GLA_HEADS
    hv = gw // GLA_HEADS
    ns = seq // ts
    row = lambda b, s: b * ns + s
    return pl.pallas_call(
        functools.partial(_gla_kernel, scale=hk ** -0.5, n_chunks=ts // CHUNK),
        grid=(batch, ns),
        in_specs=[pl.BlockSpec((ts, kd), lambda b, s: (row(b, s), 0)),
                  pl.BlockSpec((ts, kd), lambda b, s: (row(b, s), 1)),
                  pl.BlockSpec((ts, gw), lambda b, s: (row(b, s), 1)),
                  pl.BlockSpec((ts, gw), lambda b, s: (row(b, s), 2)),
                  pl.BlockSpec((ts, kd), lambda b, s: (row(b, s), 0)),
                  pl.BlockSpec((1, hv), lambda b, s: (0, 0))],
        out_specs=pl.BlockSpec((ts, gw), lambda b, s: (row(b, s), 0)),
        out_shape=jax.ShapeDtypeStruct((t, gw), BF16),
        scratch_shapes=[pltpu.VMEM((GLA_HEADS, hv, hk), F32)],
        compiler_params=_params(("parallel", "arbitrary")),
        name="gla",
    )(proj, proj, proj, proj, g, nw)


def _hgrn_kernel(q_ref, f_ref, i_ref, r_ref, lb_ref, nw_ref, o_ref, st_ref, *, n_chunks):
    @pl.when(pl.program_id(1) == 0)
    def _():
        st_ref[...] = jnp.zeros_like(st_ref)

    masks = _score_masks(CHUNK)
    n_heads = st_ref.shape[0]
    lb = lb_ref[...]
    nw = nw_ref[...]

    def body(ci, carry):
        rows = pl.ds(pl.multiple_of(ci * CHUNK, CHUNK), CHUNK)
        zf = f_ref[rows, :].astype(F32)
        forget = lb + (1.0 - lb) * jax.nn.sigmoid(zf)
        k = (1.0 - lb) * jax.nn.sigmoid(-zf)
        qz = q_ref[rows, :].astype(F32)
        q = qz * jax.nn.sigmoid(qz)
        _recurrence_chunk(q, k, forget, i_ref[rows, :], r_ref[rows, :].astype(F32), nw,
                          st_ref, o_ref, rows, n_heads, HGRN_EXPAND, HGRN_EXPAND, masks)
        return carry

    lax.fori_loop(0, n_chunks, body, 0, unroll=2)


def _hgrn(proj, lb, nw, batch, seq, ts, hw, col0):
    t, _ = proj.shape
    ns = seq // ts
    n_heads = hw // HGRN_EXPAND
    row = lambda b, s: b * ns + s
    return pl.pallas_call(
        functools.partial(_hgrn_kernel, n_chunks=ts // CHUNK),
        grid=(batch, ns),
        in_specs=[pl.BlockSpec((ts, hw), lambda b, s: (row(b, s), col0)),
                  pl.BlockSpec((ts, hw), lambda b, s: (row(b, s), col0 + 1)),
                  pl.BlockSpec((ts, hw), lambda b, s: (row(b, s), col0 + 2)),
                  pl.BlockSpec((ts, hw), lambda b, s: (row(b, s), col0 + 3)),
                  pl.BlockSpec((1, hw), lambda b, s: (0, 0)),
                  pl.BlockSpec((1, HGRN_EXPAND), lambda b, s: (0, 0))],
        out_specs=pl.BlockSpec((ts, hw), lambda b, s: (row(b, s), 0)),
        out_shape=jax.ShapeDtypeStruct((t, hw), BF16),
        scratch_shapes=[pltpu.VMEM((n_heads, HGRN_EXPAND, HGRN_EXPAND), F32)],
        compiler_params=_params(("parallel", "arbitrary")),
        name="hgrn",
    )(proj, proj, proj, proj, lb, nw)


def _out_proj_kernel(oa_ref, ob_ref, wa_ref, wb_ref, x_ref, h_ref):
    h_ref[...] = x_ref[...] + _dot(oa_ref[...], wa_ref[...]) + _dot(ob_ref[...], wb_ref[...])


def _out_proj(oa, ob, w, x2, tm, tn):
    t, d = x2.shape
    ka, kb = oa.shape[1], ob.shape[1]
    assert ka == kb and w.shape[0] == ka + kb
    return pl.pallas_call(
        _out_proj_kernel,
        grid=(d // tn, t // tm),
        in_specs=[pl.BlockSpec((tm, ka), lambda j, i: (i, 0)),
                  pl.BlockSpec((tm, kb), lambda j, i: (i, 0)),
                  pl.BlockSpec((ka, tn), lambda j, i: (0, j)),
                  pl.BlockSpec((kb, tn), lambda j, i: (1, j)),
                  pl.BlockSpec((tm, tn), lambda j, i: (i, j))],
        out_specs=pl.BlockSpec((tm, tn), lambda j, i: (i, j)),
        out_shape=jax.ShapeDtypeStruct((t, d), F32),
        compiler_params=_params(("parallel", "parallel")),
        name="out_proj",
    )(oa, ob, w, w, x2)


def _router_kernel(h_ref, nw_ref, rw_ref, rb_ref, topi_ref, topw_ref, rank_ref, cnt_ref, run_ref):
    step = pl.program_id(0)

    @pl.when(step == 0)
    def _():
        run_ref[...] = jnp.zeros_like(run_ref)

    tr = h_ref.shape[0]
    xn = _rmsnorm(h_ref[...], nw_ref[...])
    logits = jnp.dot(xn, rw_ref[...], precision=lax.Precision.HIGHEST,
                     preferred_element_type=F32) + rb_ref[...]
    lane = lax.broadcasted_iota(I32, (tr, LANES), 1)
    neg = jnp.float32(-jnp.inf)
    cur = jnp.where(lane < N_EXPERTS, logits, neg)
    vals, sels = [], []
    topi = jnp.zeros((tr, LANES), I32)
    for kk in range(TOP_K):
        m = jnp.max(cur, axis=-1, keepdims=True)
        idx = jnp.min(jnp.where(cur == m, lane, LANES), axis=-1, keepdims=True)
        sel = lane == idx
        vals.append(m)
        sels.append(sel)
        topi = jnp.where(lane == kk, idx, topi)
        cur = jnp.where(sel, neg, cur)
    exps = [jnp.exp(v - vals[0]) for v in vals]
    denom = exps[0] + exps[1] + exps[2] + exps[3]
    topw = jnp.zeros((tr, LANES), F32)
    hot = jnp.zeros((tr, LANES), F32)
    for kk in range(TOP_K):
        topw = jnp.where(lane == kk, exps[kk] / denom, topw)
        hot = hot + sels[kk].astype(F32)
    r_i = lax.broadcasted_iota(I32, (tr, tr), 0)
    c_i = lax.broadcasted_iota(I32, (tr, tr), 1)
    lower = (c_i < r_i).astype(BF16)
    before = _dot(lower, hot.astype(BF16)) + run_ref[...]
    rank = jnp.zeros((tr, LANES), F32)
    for kk in range(TOP_K):
        rk = jnp.sum(jnp.where(sels[kk], before, 0.0), axis=-1, keepdims=True)
        rank = jnp.where(lane == kk, rk, rank)
    topi_ref[...] = topi
    topw_ref[...] = topw
    rank_ref[...] = rank.astype(I32)
    run_ref[...] = run_ref[...] + jnp.sum(hot, axis=0, keepdims=True)
    cnt_ref[...] = run_ref[...]


def _router(h1, nw, rw, rb, tr):
    t, d = h1.shape
    tile = lambda i: (i, 0)
    fixed = lambda i: (0, 0)
    return pl.pallas_call(
        _router_kernel,
        grid=(t // tr,),
        in_specs=[pl.BlockSpec((tr, d), tile),
                  pl.BlockSpec((1, d), fixed),
                  pl.BlockSpec((d, LANES), fixed),
                  pl.BlockSpec((1, LANES), fixed)],
        out_specs=[pl.BlockSpec((tr, LANES), tile),
                   pl.BlockSpec((tr, LANES), tile),
                   pl.BlockSpec((tr, LANES), tile),
                   pl.BlockSpec((1, LANES), fixed)],
        out_shape=[jax.ShapeDtypeStruct((t, LANES), I32),
                   jax.ShapeDtypeStruct((t, LANES), F32),
                   jax.ShapeDtypeStruct((t, LANES), I32),
                   jax.ShapeDtypeStruct((1, LANES), F32)],
        scratch_shapes=[pltpu.VMEM((1, LANES), F32)],
        compiler_params=_params(("arbitrary",)),
        name="router",
    )(h1, nw, rw, rb)


def _dispatch_kernel(pos_ref, h_ref, nw_ref, xs_in_ref, xs_ref, buf_ref, sem):
    del xs_in_ref
    td = h_ref.shape[0]
    buf_ref[...] = _pack_bf16_pairs(_rmsnorm(h_ref[...], nw_ref[...]))

    def issue(ti, carry):
        for kk in range(TOP_K):
            p = pos_ref[0, 0, kk * td + ti]
            pltpu.make_async_copy(buf_ref.at[pl.ds(ti, 1)], xs_ref.at[pl.ds(p, 1)], sem).start()
        return carry

    lax.fori_loop(0, td, issue, 0)
    for kk in range(TOP_K):
        pltpu.make_async_copy(buf_ref, xs_ref.at[pl.ds(0, td)], sem).wait()


def _dispatch(pos_tiles, h1, nw, xs_zero, td):
    t, d = h1.shape
    return pl.pallas_call(
        _dispatch_kernel,
        grid=(t // td,),
        in_specs=[pl.BlockSpec((1, 1, TOP_K * td), lambda i: (i, 0, 0), memory_space=pltpu.SMEM),
                  pl.BlockSpec((td, d), lambda i: (i, 0)),
                  pl.BlockSpec((1, d), lambda i: (0, 0)),
                  pl.BlockSpec(memory_space=pl.ANY)],
        out_specs=pl.BlockSpec(memory_space=pl.ANY),
        out_shape=jax.ShapeDtypeStruct(xs_zero.shape, U32),
        scratch_shapes=[pltpu.VMEM((td, d // 2), U32), pltpu.SemaphoreType.DMA],
        input_output_aliases={3: 0},
        compiler_params=_params(("arbitrary",)),
        name="dispatch",
    )(pos_tiles, h1, nw, xs_zero)


def _experts_kernel(te_ref, tb_ref, nu_ref, x_ref, gw_ref, gb_ref, uw_ref, ub_ref,
                    dlo_ref, dhi_ref, blo_ref, bhi_ref, y_ref, xb_ref, hid_ref, *, nf):
    del te_ref, tb_ref
    i = pl.program_id(0)
    s = pl.program_id(1)
    half = x_ref.shape[1]
    tf = gw_ref.shape[2]
    active = i < nu_ref[0]

    @pl.when(active & (s == 0))
    def _():
        u = x_ref[...]
        xb_ref[:, :half] = _unpack_lo(u).astype(BF16)
        xb_ref[:, half:] = _unpack_hi(u).astype(BF16)

    @pl.when(active & (s < nf))
    def _():
        xb = xb_ref[...]
        g = jnp.minimum(_dot(xb, gw_ref[0].astype(BF16)) + gb_ref[0], SWIGLU_LIMIT)
        u = jnp.clip(_dot(xb, uw_ref[0].astype(BF16)) + ub_ref[0], -SWIGLU_LIMIT, SWIGLU_LIMIT)
        hid_ref[s] = (g * jax.nn.sigmoid(SWIGLU_ALPHA * g) * (u + 1.0)).astype(BF16)

    @pl.when(active & (s >= nf))
    def _():
        lo = blo_ref[0]
        hi = bhi_ref[0]
        for f in range(nf):
            hf = hid_ref[f]
            lo = lo + _dot(hf, dlo_ref[0, f * tf:(f + 1) * tf, :].astype(BF16))
            hi = hi + _dot(hf, dhi_ref[0, f * tf:(f + 1) * tf, :].astype(BF16))
        y_ref[...] = _pack2(lo, hi)

    @pl.when(jnp.logical_not(active) & (s >= nf))
    def _():
        y_ref[...] = jnp.zeros_like(y_ref)


def _experts(tile_expert, tile_block, n_used, xs, gw, gb, uw, ub, dw, db, tm, tf, tn):
    p, half = xs.shape
    d = 2 * half
    ff = gw.shape[2]
    nf = ff // tf
    nn = half // tn
    n_tiles = p // tm

    def fidx(i, s, nu):
        return jnp.where(i < nu[0], jnp.minimum(s, nf - 1), nf - 1)

    def nidx(i, s, nu):
        return jnp.where(i < nu[0], jnp.maximum(s - nf, 0), nn - 1)

    grid_spec = pltpu.PrefetchScalarGridSpec(
        num_scalar_prefetch=3,
        grid=(n_tiles, nf + nn),
        in_specs=[pl.BlockSpec((tm, half), lambda i, s, te, tb, nu: (tb[i], 0)),
                  pl.BlockSpec((1, d, tf), lambda i, s, te, tb, nu: (te[i], 0, fidx(i, s, nu))),
                  pl.BlockSpec((1, 1, tf), lambda i, s, te, tb, nu: (te[i], 0, fidx(i, s, nu))),
                  pl.BlockSpec((1, d, tf), lambda i, s, te, tb, nu: (te[i], 0, fidx(i, s, nu))),
                  pl.BlockSpec((1, 1, tf), lambda i, s, te, tb, nu: (te[i], 0, fidx(i, s, nu))),
                  pl.BlockSpec((1, ff, tn), lambda i, s, te, tb, nu: (te[i], 0, nidx(i, s, nu))),
                  pl.BlockSpec((1, ff, tn), lambda i, s, te, tb, nu: (te[i], 0, nn + nidx(i, s, nu))),
                  pl.BlockSpec((1, 1, tn), lambda i, s, te, tb, nu: (te[i], 0, nidx(i, s, nu))),
                  pl.BlockSpec((1, 1, tn), lambda i, s, te, tb, nu: (te[i], 0, nn + nidx(i, s, nu)))],
        out_specs=pl.BlockSpec((tm, tn), lambda i, s, te, tb, nu: (i, jnp.maximum(s - nf, 0))),
        scratch_shapes=[pltpu.VMEM((tm, d), BF16), pltpu.VMEM((nf, tm, tf), BF16)],
    )
    return pl.pallas_call(
        functools.partial(_experts_kernel, nf=nf),
        grid_spec=grid_spec,
        out_shape=jax.ShapeDtypeStruct((p, half), U32),
        compiler_params=_params(("arbitrary", "arbitrary")),
        name="experts",
    )(tile_expert, tile_block, n_used, xs, gw, gb, uw, ub, dw, dw, db, db)


def _combine_kernel(pos_ref, h_ref, w_ref, nw_ref, y_ref, o_ref, buf_ref, sem):
    tc = h_ref.shape[0]
    half = buf_ref.shape[2]

    def issue(ti, carry):
        for kk in range(TOP_K):
            p = pos_ref[0, 0, kk * tc + ti]
            pltpu.make_async_copy(y_ref.at[pl.ds(p, 1)], buf_ref.at[kk, pl.ds(ti, 1)], sem).start()
        return carry

    lax.fori_loop(0, tc, issue, 0)
    for kk in range(TOP_K):
        pltpu.make_async_copy(y_ref.at[pl.ds(0, tc)], buf_ref.at[kk], sem).wait()

    w = w_ref[...]
    h = h_ref[...]
    lo = h[:, :half]
    hi = h[:, half:]
    for kk in range(TOP_K):
        u = buf_ref[kk]
        wk = w[:, kk:kk + 1]
        lo = lo + wk * _unpack_lo(u)
        hi = hi + wk * _unpack_hi(u)
    ms = (jnp.sum(lo * lo, axis=-1, keepdims=True) + jnp.sum(hi * hi, axis=-1, keepdims=True)) / (2 * half)
    inv = lax.rsqrt(ms + EPS)
    nw = nw_ref[...]
    o_ref[:, :half] = lo * inv * nw[:, :half]
    o_ref[:, half:] = hi * inv * nw[:, half:]


def _combine(pos_tiles, h1, topw, nw, y, tc):
    t, d = h1.shape
    return pl.pallas_call(
        _combine_kernel,
        grid=(t // tc,),
        in_specs=[pl.BlockSpec((1, 1, TOP_K * tc), lambda i: (i, 0, 0), memory_space=pltpu.SMEM),
                  pl.BlockSpec((tc, d), lambda i: (i, 0)),
                  pl.BlockSpec((tc, LANES), lambda i: (i, 0)),
                  pl.BlockSpec((1, d), lambda i: (0, 0)),
                  pl.BlockSpec(memory_space=pl.ANY)],
        out_specs=pl.BlockSpec((tc, d), lambda i: (i, 0)),
        out_shape=jax.ShapeDtypeStruct((t, d), F32),
        scratch_shapes=[pltpu.VMEM((TOP_K, tc, d // 2), U32), pltpu.SemaphoreType.DMA],
        compiler_params=_params(("arbitrary",)),
        name="combine",
    )(pos_tiles, h1, topw, nw, y)


def _tiles(t, seq, d):
    return dict(
        norm_tm=min(512, t),
        proj_tm=min(1024, t), proj_tn=min(1024, d // 2),
        rec_ts=min(512, seq),
        out_tm=min(512, t), out_tn=min(1024, d),
        router_tr=min(512, t),
        scatter_td=min(256, t),
        expert_tm=min(512, t), expert_tf=min(256, d // 4), expert_tn=min(1024, d // 2),
        combine_tc=min(256, t),
    )


def _pos_tiles(pos, tile):
    t = pos.shape[0]
    return pos.reshape(t // tile, tile, TOP_K).transpose(0, 2, 1).reshape(t // tile, 1, TOP_K * tile)


def kernel(x, attn_norm_w, in_proj_w, gla_gate_up_w, gla_gate_up_b, gla_out_norm_w, hgrn_lb_logits, hgrn_out_norm_w, out_proj_w, ffn_norm_w, router_w, router_b, expert_gate_w, expert_gate_b, expert_up_w, expert_up_b, expert_down_w, expert_down_b, final_norm_w):
    batch, seq, d = x.shape
    depth = attn_norm_w.shape[0]
    assert depth == 1 and d % 2048 == 0 and seq % CHUNK == 0
    t = batch * seq
    kd = d // 4
    gw = d // 2
    hw = d - gw
    ff = expert_gate_w.shape[-1]
    ts = _tiles(t, seq, d)

    lower = jnp.cumsum(jax.nn.softmax(hgrn_lb_logits.astype(F32), axis=0), axis=0)[0:1]

    w_in = in_proj_w[0]
    c_lr = 2 * kd + 2 * gw
    w_main = jnp.concatenate([w_in[:, :c_lr], w_in[:, c_lr + GLA_RANK:]], axis=1).astype(BF16)
    w_lr = jnp.pad(w_in[:, c_lr:c_lr + GLA_RANK], ((0, 0), (0, LANES - GLA_RANK))).astype(BF16)
    up_w = jnp.pad(gla_gate_up_w[0], ((0, LANES - GLA_RANK), (0, 0)))
    w_out = out_proj_w[0].astype(BF16)
    r_w = jnp.pad(router_w[0], ((0, 0), (0, LANES - N_EXPERTS)))
    r_b = jnp.pad(router_b, ((0, 0), (0, LANES - N_EXPERTS)))

    x2 = x.reshape(t, d)
    xn, g_gla = _norm_gate(x2, attn_norm_w, w_lr, up_w, gla_gate_up_b, ts["norm_tm"])
    proj = _in_proj(xn, w_main, ts["proj_tm"], ts["proj_tn"])
    o_a = _gla(proj, g_gla, gla_out_norm_w, batch, seq, ts["rec_ts"])
    o_b = _hgrn(proj, lower, hgrn_out_norm_w, batch, seq, ts["rec_ts"], hw, c_lr // hw)
    h1 = _out_proj(o_a, o_b, w_out, x2, ts["out_tm"], ts["out_tn"])

    topi, topw, rank, counts = _router(h1, ffn_norm_w, r_w, r_b, ts["router_tr"])

    tm = ts["expert_tm"]
    n_tiles = (t * TOP_K) // tm + N_EXPERTS
    cnt = counts[0, :N_EXPERTS].astype(I32)
    tiles_e = (cnt + tm - 1) // tm
    tile_end = jnp.cumsum(tiles_e)
    row_start = (tile_end - tiles_e) * tm
    n_used = tile_end[-1]
    pos = row_start[topi[:, :TOP_K]] + rank[:, :TOP_K]
    tile_id = jnp.arange(n_tiles, dtype=I32)
    tile_block = jnp.minimum(tile_id, n_used - 1)
    tile_expert = jnp.sum((tile_block[:, None] >= tile_end[None, :]).astype(I32), axis=1)

    xs_zero = jnp.zeros((n_tiles * tm, d // 2), U32)
    xs = _dispatch(_pos_tiles(pos, ts["scatter_td"]), h1, ffn_norm_w, xs_zero, ts["scatter_td"])
    y = _experts(tile_expert, tile_block, n_used.reshape(1), xs,
                 expert_gate_w[0], expert_gate_b[0][:, None, :], expert_up_w[0], expert_up_b[0][:, None, :],
                 expert_down_w[0], expert_down_b[0][:, None, :], tm, ts["expert_tf"], ts["expert_tn"])
    out = _combine(_pos_tiles(pos, ts["combine_tc"]), h1, topw, final_norm_w[None, :], y, ts["combine_tc"])
    return out.reshape(batch, seq, d)
```

```python
import functools

import jax
import jax.numpy as jnp
from jax import lax
from jax.experimental import pallas as pl
from jax.experimental.pallas import tpu as pltpu

CHUNK = 64
N_EXPERTS = 32
TOP_K = 4
GLA_HEADS = 4
GLA_RANK = 16
GLA_GATE_NORM = 16.0
HGRN_EXPAND = 128
SWIGLU_ALPHA = 1.702
SWIGLU_LIMIT = 7.0
EPS = 1e-5
LANES = 128
VMEM_LIMIT = 56 * 1024 * 1024

F32 = jnp.float32
BF16 = jnp.bfloat16
U32 = jnp.uint32
I32 = jnp.int32


def _params(semantics):
    return pltpu.CompilerParams(dimension_semantics=semantics, vmem_limit_bytes=VMEM_LIMIT)


def _dot(a, b):
    return jnp.dot(a, b, preferred_element_type=F32)


def _dot_nt(a, b):
    return lax.dot_general(a, b, (((1,), (1,)), ((), ())), preferred_element_type=F32)


def _dot_tn(a, b):
    return lax.dot_general(a, b, (((0,), (0,)), ((), ())), preferred_element_type=F32)


def _rmsnorm(x, w):
    return x * lax.rsqrt(jnp.mean(x * x, axis=-1, keepdims=True) + EPS) * w


def _pack_bf16_pairs(x):
    n = x.shape[1] // 2
    return _pack2(x[:, :n], x[:, n:])


def _pack2(lo, hi):
    return pltpu.pack_elementwise([lo, hi], packed_dtype=BF16)


def _unpack_lo(u):
    return pltpu.unpack_elementwise(u, index=0, packed_dtype=BF16, unpacked_dtype=F32)


def _unpack_hi(u):
    return pltpu.unpack_elementwise(u, index=1, packed_dtype=BF16, unpacked_dtype=F32)


def _norm_gate_kernel(x_ref, nw_ref, wlr_ref, upw_ref, upb_ref, xn_ref, g_ref):
    xn = _rmsnorm(x_ref[...], nw_ref[...]).astype(BF16)
    xn_ref[...] = xn
    lr = _dot(xn, wlr_ref[...])
    z = jnp.dot(lr, upw_ref[...], precision=lax.Precision.HIGHEST,
                preferred_element_type=F32) + upb_ref[...]
    log_sig = jnp.minimum(z, 0.0) - jnp.log1p(jnp.exp(-jnp.abs(z)))
    g_ref[...] = log_sig * (1.0 / GLA_GATE_NORM)


def _norm_gate(x2, nw, wlr, upw, upb, tm):
    t, d = x2.shape
    kd = upw.shape[1]
    return pl.pallas_call(
        _norm_gate_kernel,
        grid=(t // tm,),
        in_specs=[pl.BlockSpec((tm, d), lambda i: (i, 0)),
                  pl.BlockSpec((1, d), lambda i: (0, 0)),
                  pl.BlockSpec((d, LANES), lambda i: (0, 0)),
                  pl.BlockSpec((LANES, kd), lambda i: (0, 0)),
                  pl.BlockSpec((1, kd), lambda i: (0, 0))],
        out_specs=[pl.BlockSpec((tm, d), lambda i: (i, 0)),
                   pl.BlockSpec((tm, kd), lambda i: (i, 0))],
        out_shape=[jax.ShapeDtypeStruct((t, d), BF16), jax.ShapeDtypeStruct((t, kd), F32)],
        compiler_params=_params(("parallel",)),
        name="norm_gate",
    )(x2, nw, wlr, upw, upb)


def _matmul_kernel(a_ref, b_ref, o_ref):
    o_ref[...] = _dot(a_ref[...], b_ref[...]).astype(o_ref.dtype)


def _in_proj(xn, w, tm, tn):
    t, d = xn.shape
    n = w.shape[1]
    return pl.pallas_call(
        _matmul_kernel,
        grid=(n // tn, t // tm),
        in_specs=[pl.BlockSpec((tm, d), lambda j, i: (i, 0)),
                  pl.BlockSpec((d, tn), lambda j, i: (0, j))],
        out_specs=pl.BlockSpec((tm, tn), lambda j, i: (i, j)),
        out_shape=jax.ShapeDtypeStruct((t, n), BF16),
        compiler_params=_params(("parallel", "parallel")),
        name="in_proj",
    )(xn, w)


def _decay_levels(w):
    c = w.shape[0]
    row = lax.broadcasted_iota(I32, (c, 1), 0)
    pre, suf, tot = w, None, w
    levels = []
    s = 1
    while s < 8:
        levels.append((pre, suf))
        odd = (row & s) != 0
        prev_tot = pltpu.roll(tot, s, 0)
        next_tot = pltpu.roll(tot, c - s, 0)
        pre = pre * jnp.where(odd, prev_tot, 1.0)
        grow = jnp.where(odd, 1.0, next_tot)
        suf = grow if suf is None else suf * grow
        tot = tot * jnp.where(odd, prev_tot, next_tot)
        s *= 2
    groups = c // 8
    pre_g = [pre[8 * r:8 * r + 8] for r in range(groups)]
    suf_g = [suf[8 * r:8 * r + 8] for r in range(groups)]
    tot_g = [tot[8 * r:8 * r + 8] for r in range(groups)]
    m = 1
    while 8 * m < c:
        levels.append((jnp.concatenate(pre_g, axis=0), jnp.concatenate(suf_g, axis=0)))
        new_tot = []
        for r in range(groups):
            if (r // m) & 1:
                pre_g[r] = pre_g[r] * tot_g[r - m]
                new_tot.append(tot_g[r] * tot_g[r - m])
            else:
                suf_g[r] = suf_g[r] * tot_g[r + m]
                new_tot.append(tot_g[r] * tot_g[r + m])
        tot_g = new_tot
        m *= 2
    return levels, jnp.concatenate(pre_g, axis=0), jnp.concatenate(suf_g, axis=0), tot_g[0][0:1]


def _score_masks(c):
    row = lax.broadcasted_iota(I32, (c, c), 0)
    col = lax.broadcasted_iota(I32, (c, c), 1)
    masks = [row == col]
    s, sh = 1, 0
    while s < c:
        masks.append((((row ^ col) >> sh) == 1) & ((row & s) != 0))
        s *= 2
        sh += 1
    return masks


def _recurrence_chunk(q, k, w, v, r, nw, st_ref, o_ref, rows, n_heads, dk, dv, masks):
    levels, pre_c, suf_c, dec = _decay_levels(w)
    kb = k.astype(BF16)
    ops = [(q.astype(BF16), kb)]
    for pre, suf in levels:
        ops.append(((q * pre).astype(BF16), kb if suf is None else (k * suf).astype(BF16)))
    qd = (q * pre_c).astype(BF16)
    kd = (k * suf_c).astype(BF16)
    for h in range(n_heads):
        sk = slice(h * dk, (h + 1) * dk)
        sv = slice(h * dv, (h + 1) * dv)
        a = None
        for (qs, ks), m in zip(ops, masks):
            part = jnp.where(m, _dot_nt(qs[:, sk], ks[:, sk]), 0.0)
            a = part if a is None else a + part
        st = st_ref[h]
        vh = v[:, sv]
        o = _dot_nt(qd[:, sk], st.astype(BF16)) + _dot(a.astype(BF16), vh)
        st_ref[h] = st * dec[:, sk] + _dot_tn(vh, kd[:, sk])
        y = o * lax.rsqrt(jnp.mean(o * o, axis=-1, keepdims=True) + EPS) * nw
        rh = r[:, sv]
        o_ref[rows, sv] = (y * (rh * jax.nn.sigmoid(rh))).astype(o_ref.dtype)


def _gla_kernel(q_ref, k_ref, v_ref, r_ref, g_ref, nw_ref, o_ref, st_ref, *, scale, n_chunks):
    @pl.when(pl.program_id(1) == 0)
    def _():
        st_ref[...] = jnp.zeros_like(st_ref)

    masks = _score_masks(CHUNK)
    n_heads, dv, dk = st_ref.shape
    nw = nw_ref[...]

    def body(ci, carry):
        rows = pl.ds(pl.multiple_of(ci * CHUNK, CHUNK), CHUNK)
        q = q_ref[rows, :].astype(F32) * scale
        k = k_ref[rows, :].astype(F32)
        _recurrence_chunk(q, k, jnp.exp(g_ref[rows, :]), v_ref[rows, :], r_ref[rows, :].astype(F32), nw,
                          st_ref, o_ref, rows, n_heads, dk, dv, masks)
        return carry

    lax.fori_loop(0, n_chunks, body, 0, unroll=2)


def _gla(proj, g, nw, batch, seq, ts):
    t, _ = proj.shape
    kd = g.shape[1]
    gw = 2 * kd
    hk = kd // GLA_HEADS
    hv = gw // GLA_HEADS
    ns = seq // ts
    row = lambda b, s: b * ns + s
    return pl.pallas_call(
        functools.partial(_gla_kernel, scale=hk ** -0.5, n_chunks=ts // CHUNK),
        grid=(batch, ns),
        in_specs=[pl.BlockSpec((ts, kd), lambda b, s: (row(b, s), 0)),
                  pl.BlockSpec((ts, kd), lambda b, s: (row(b, s), 1)),
                  pl.BlockSpec((ts, gw), lambda b, s: (row(b, s), 1)),
                  pl.BlockSpec((ts, gw), lambda b, s: (row(b, s), 2)),
                  pl.BlockSpec((ts, kd), lambda b, s: (row(b, s), 0)),
                  pl.BlockSpec((1, hv), lambda b, s: (0, 0))],
        out_specs=pl.BlockSpec((ts, gw), lambda b, s: (row(b, s), 0)),
        out_shape=jax.ShapeDtypeStruct((t, gw), BF16),
        scratch_shapes=[pltpu.VMEM((GLA_HEADS, hv, hk), F32)],
        compiler_params=_params(("parallel", "arbitrary")),
        name="gla",
    )(proj, proj, proj, proj, g, nw)


def _hgrn_kernel(q_ref, f_ref, i_ref, r_ref, lb_ref, nw_ref, o_ref, st_ref, *, n_chunks):
    @pl.when(pl.program_id(1) == 0)
    def _():
        st_ref[...] = jnp.zeros_like(st_ref)

    masks = _score_masks(CHUNK)
    n_heads = st_ref.shape[0]
    lb = lb_ref[...]
    nw = nw_ref[...]

    def body(ci, carry):
        rows = pl.ds(pl.multiple_of(ci * CHUNK, CHUNK), CHUNK)
        zf = f_ref[rows, :].astype(F32)
        forget = lb + (1.0 - lb) * jax.nn.sigmoid(zf)
        k = (1.0 - lb) * jax.nn.sigmoid(-zf)
        qz = q_ref[rows, :].astype(F32)
        q = qz * jax.nn.sigmoid(qz)
        _recurrence_chunk(q, k, forget, i_ref[rows, :], r_ref[rows, :].astype(F32), nw,
                          st_ref, o_ref, rows, n_heads, HGRN_EXPAND, HGRN_EXPAND, masks)
        return carry

    lax.fori_loop(0, n_chunks, body, 0, unroll=2)


def _hgrn(proj, lb, nw, batch, seq, ts, hw, col0):
    t, _ = proj.shape
    ns = seq // ts
    n_heads = hw // HGRN_EXPAND
    row = lambda b, s: b * ns + s
    return pl.pallas_call(
        functools.partial(_hgrn_kernel, n_chunks=ts // CHUNK),
        grid=(batch, ns),
        in_specs=[pl.BlockSpec((ts, hw), lambda b, s: (row(b, s), col0)),
                  pl.BlockSpec((ts, hw), lambda b, s: (row(b, s), col0 + 1)),
                  pl.BlockSpec((ts, hw), lambda b, s: (row(b, s), col0 + 2)),
                  pl.BlockSpec((ts, hw), lambda b, s: (row(b, s), col0 + 3)),
                  pl.BlockSpec((1, hw), lambda b, s: (0, 0)),
                  pl.BlockSpec((1, HGRN_EXPAND), lambda b, s: (0, 0))],
        out_specs=pl.BlockSpec((ts, hw), lambda b, s: (row(b, s), 0)),
        out_shape=jax.ShapeDtypeStruct((t, hw), BF16),
        scratch_shapes=[pltpu.VMEM((n_heads, HGRN_EXPAND, HGRN_EXPAND), F32)],
        compiler_params=_params(("parallel", "arbitrary")),
        name="hgrn",
    )(proj, proj, proj, proj, lb, nw)


def _out_proj_kernel(oa_ref, ob_ref, wa_ref, wb_ref, x_ref, h_ref):
    h_ref[...] = x_ref[...] + _dot(oa_ref[...], wa_ref[...]) + _dot(ob_ref[...], wb_ref[...])


def _out_proj(oa, ob, w, x2, tm, tn):
    t, d = x2.shape
    ka, kb = oa.shape[1], ob.shape[1]
    assert ka == kb and w.shape[0] == ka + kb
    return pl.pallas_call(
        _out_proj_kernel,
        grid=(d // tn, t // tm),
        in_specs=[pl.BlockSpec((tm, ka), lambda j, i: (i, 0)),
                  pl.BlockSpec((tm, kb), lambda j, i: (i, 0)),
                  pl.BlockSpec((ka, tn), lambda j, i: (0, j)),
                  pl.BlockSpec((kb, tn), lambda j, i: (1, j)),
                  pl.BlockSpec((tm, tn), lambda j, i: (i, j))],
        out_specs=pl.BlockSpec((tm, tn), lambda j, i: (i, j)),
        out_shape=jax.ShapeDtypeStruct((t, d), F32),
        compiler_params=_params(("parallel", "parallel")),
        name="out_proj",
    )(oa, ob, w, w, x2)


def _router_kernel(h_ref, nw_ref, rw_ref, rb_ref, topi_ref, topw_ref, rank_ref, cnt_ref, run_ref):
    step = pl.program_id(0)

    @pl.when(step == 0)
    def _():
        run_ref[...] = jnp.zeros_like(run_ref)

    tr = h_ref.shape[0]
    xn = _rmsnorm(h_ref[...], nw_ref[...])
    logits = jnp.dot(xn, rw_ref[...], precision=lax.Precision.HIGHEST,
                     preferred_element_type=F32) + rb_ref[...]
    lane = lax.broadcasted_iota(I32, (tr, LANES), 1)
    neg = jnp.float32(-jnp.inf)
    cur = jnp.where(lane < N_EXPERTS, logits, neg)
    vals, sels = [], []
    topi = jnp.zeros((tr, LANES), I32)
    for kk in range(TOP_K):
        m = jnp.max(cur, axis=-1, keepdims=True)
        idx = jnp.min(jnp.where(cur == m, lane, LANES), axis=-1, keepdims=True)
        sel = lane == idx
        vals.append(m)
        sels.append(sel)
        topi = jnp.where(lane == kk, idx, topi)
        cur = jnp.where(sel, neg, cur)
    exps = [jnp.exp(v - vals[0]) for v in vals]
    denom = exps[0] + exps[1] + exps[2] + exps[3]
    topw = jnp.zeros((tr, LANES), F32)
    hot = jnp.zeros((tr, LANES), F32)
    for kk in range(TOP_K):
        topw = jnp.where(lane == kk, exps[kk] / denom, topw)
        hot = hot + sels[kk].astype(F32)
    r_i = lax.broadcasted_iota(I32, (tr, tr), 0)
    c_i = lax.broadcasted_iota(I32, (tr, tr), 1)
    lower = (c_i < r_i).astype(BF16)
    before = _dot(lower, hot.astype(BF16)) + run_ref[...]
    rank = jnp.zeros((tr, LANES), F32)
    for kk in range(TOP_K):
        rk = jnp.sum(jnp.where(sels[kk], before, 0.0), axis=-1, keepdims=True)
        rank = jnp.where(lane == kk, rk, rank)
    topi_ref[...] = topi
    topw_ref[...] = topw
    rank_ref[...] = rank.astype(I32)
    run_ref[...] = run_ref[...] + jnp.sum(hot, axis=0, keepdims=True)
    cnt_ref[...] = run_ref[...]


def _router(h1, nw, rw, rb, tr):
    t, d = h1.shape
    tile = lambda i: (i, 0)
    fixed = lambda i: (0, 0)
    return pl.pallas_call(
        _router_kernel,
        grid=(t // tr,),
        in_specs=[pl.BlockSpec((tr, d), tile),
                  pl.BlockSpec((1, d), fixed),
                  pl.BlockSpec((d, LANES), fixed),
                  pl.BlockSpec((1, LANES), fixed)],
        out_specs=[pl.BlockSpec((tr, LANES), tile),
                   pl.BlockSpec((tr, LANES), tile),
                   pl.BlockSpec((tr, LANES), tile),
                   pl.BlockSpec((1, LANES), fixed)],
        out_shape=[jax.ShapeDtypeStruct((t, LANES), I32),
                   jax.ShapeDtypeStruct((t, LANES), F32),
                   jax.ShapeDtypeStruct((t, LANES), I32),
                   jax.ShapeDtypeStruct((1, LANES), F32)],
        scratch_shapes=[pltpu.VMEM((1, LANES), F32)],
        compiler_params=_params(("arbitrary",)),
        name="router",
    )(h1, nw, rw, rb)


def _zero_rows(pad_ref, xs_ref, zero_ref, sem, n_spans, wait):
    zr = zero_ref.shape[0]

    def run(copy):
        copy.wait() if wait else copy.start()

    def span(e, carry):
        start = pad_ref[0, 0, e]

        def one(j, c2):
            run(pltpu.make_async_copy(zero_ref.at[pl.ds(0, 1)], xs_ref.at[pl.ds(start + j, 1)], sem))
            return c2

        lax.fori_loop(0, pad_ref[0, 0, n_spans + e], one, 0)
        return carry

    lax.fori_loop(0, n_spans - 1, span, 0)
    tail = pad_ref[0, 0, n_spans - 1]

    def big(j, carry):
        run(pltpu.make_async_copy(zero_ref, xs_ref.at[pl.ds(pl.multiple_of(tail + j * zr, zr), zr)], sem))
        return carry

    lax.fori_loop(0, pad_ref[0, 0, 2 * n_spans - 1] // zr, big, 0)


def _dispatch_kernel(pos_ref, pad_ref, h_ref, nw_ref, xs_ref, buf_ref, zero_ref, sem, zsem, *, n_spans):
    i = pl.program_id(0)
    n = pl.num_programs(0)
    td = h_ref.shape[0]
    slot = i % 2

    def wait_rows(sl):
        for _ in range(TOP_K):
            pltpu.make_async_copy(buf_ref.at[sl], xs_ref.at[pl.ds(0, td)], sem.at[sl]).wait()

    @pl.when(i == 0)
    def _():
        zeros = jnp.zeros(zero_ref.shape, F32)
        zero_ref[...] = _pack2(zeros, zeros)
        _zero_rows(pad_ref, xs_ref, zero_ref, zsem, n_spans, wait=False)

    @pl.when(i >= 2)
    def _():
        wait_rows(slot)

    buf_ref[slot] = _pack_bf16_pairs(_rmsnorm(h_ref[...], nw_ref[...]))

    def issue(ti, carry):
        for kk in range(TOP_K):
            p = pos_ref[0, 0, kk * td + ti]
            pltpu.make_async_copy(buf_ref.at[slot, pl.ds(ti, 1)], xs_ref.at[pl.ds(p, 1)], sem.at[slot]).start()
        return carry

    lax.fori_loop(0, td, issue, 0)

    @pl.when(i == 0)
    def _():
        _zero_rows(pad_ref, xs_ref, zero_ref, zsem, n_spans, wait=True)

    @pl.when(i == n - 1)
    def _():
        wait_rows(slot)

    @pl.when((i == n - 1) & (n > 1))
    def _():
        wait_rows(1 - slot)


def _dispatch(pos_tiles, pad_spans, h1, nw, n_rows, td):
    t, d = h1.shape
    n_spans = pad_spans.shape[2] // 2
    return pl.pallas_call(
        functools.partial(_dispatch_kernel, n_spans=n_spans),
        grid=(t // td,),
        in_specs=[pl.BlockSpec((1, 1, TOP_K * td), lambda i: (i, 0, 0), memory_space=pltpu.SMEM),
                  pl.BlockSpec((1, 1, 2 * n_spans), lambda i: (0, 0, 0), memory_space=pltpu.SMEM),
                  pl.BlockSpec((td, d), lambda i: (i, 0)),
                  pl.BlockSpec((1, d), lambda i: (0, 0))],
        out_specs=pl.BlockSpec(memory_space=pl.ANY),
        out_shape=jax.ShapeDtypeStruct((n_rows, d // 2), U32),
        scratch_shapes=[pltpu.VMEM((2, td, d // 2), U32), pltpu.VMEM((td, d // 2), U32),
                        pltpu.SemaphoreType.DMA((2,)), pltpu.SemaphoreType.DMA],
        compiler_params=_params(("arbitrary",)),
        name="dispatch",
    )(pos_tiles, pad_spans, h1, nw)


def _experts_kernel(te_ref, tb_ref, nu_ref, x_ref, gw_ref, uw_ref, gb_ref, ub_ref,
                    dlo_ref, dhi_ref, blo_ref, bhi_ref, y_ref, xb_ref, g_ref, u_ref, hid_ref, *, nk):
    del te_ref, tb_ref
    i = pl.program_id(0)
    s = pl.program_id(1)
    half = x_ref.shape[1]
    tk = gw_ref.shape[1]
    per_half = half // tk
    active = i < nu_ref[0]

    @pl.when(active & (s == 0))
    def _():
        u = x_ref[...]
        lo = _unpack_lo(u).astype(BF16)
        hi = _unpack_hi(u).astype(BF16)
        for j in range(per_half):
            xb_ref[j] = lo[:, j * tk:(j + 1) * tk]
            xb_ref[per_half + j] = hi[:, j * tk:(j + 1) * tk]

    @pl.when(active & (s < nk))
    def _():
        xk = xb_ref[s]
        g = _dot(xk, gw_ref[0].astype(BF16))
        u = _dot(xk, uw_ref[0].astype(BF16))

        @pl.when(s == 0)
        def _():
            g_ref[...] = g
            u_ref[...] = u

        @pl.when(s > 0)
        def _():
            g_ref[...] += g
            u_ref[...] += u

        @pl.when(s == nk - 1)
        def _():
            gg = jnp.minimum(g_ref[...] + gb_ref[0], SWIGLU_LIMIT)
            uu = jnp.clip(u_ref[...] + ub_ref[0], -SWIGLU_LIMIT, SWIGLU_LIMIT)
            hid_ref[...] = (gg * jax.nn.sigmoid(SWIGLU_ALPHA * gg) * (uu + 1.0)).astype(BF16)

    @pl.when(active & (s >= nk))
    def _():
        hid = hid_ref[...]
        lo = blo_ref[0] + _dot(hid, dlo_ref[0].astype(BF16))
        hi = bhi_ref[0] + _dot(hid, dhi_ref[0].astype(BF16))
        y_ref[...] = _pack2(lo, hi)

    @pl.when(jnp.logical_not(active) & (s >= nk))
    def _():
        zeros = jnp.zeros(y_ref.shape, F32)
        y_ref[...] = _pack2(zeros, zeros)


def _experts(tile_expert, tile_block, n_used, xs, gw, gb, uw, ub, dw, db, tm, tk, tn):
    p, half = xs.shape
    d = 2 * half
    ff = gw.shape[2]
    nk = d // tk
    nn = half // tn
    n_tiles = p // tm
    assert half % tk == 0

    def kidx(i, s, nu):
        return jnp.where(i < nu[0], jnp.minimum(s, nk - 1), nk - 1)

    def nidx(i, s, nu):
        return jnp.where(i < nu[0], jnp.maximum(s - nk, 0), nn - 1)

    grid_spec = pltpu.PrefetchScalarGridSpec(
        num_scalar_prefetch=3,
        grid=(n_tiles, nk + nn),
        in_specs=[pl.BlockSpec((tm, half), lambda i, s, te, tb, nu: (tb[i], 0)),
                  pl.BlockSpec((1, tk, ff), lambda i, s, te, tb, nu: (te[i], kidx(i, s, nu), 0)),
                  pl.BlockSpec((1, tk, ff), lambda i, s, te, tb, nu: (te[i], kidx(i, s, nu), 0)),
                  pl.BlockSpec((1, 1, ff), lambda i, s, te, tb, nu: (te[i], 0, 0)),
                  pl.BlockSpec((1, 1, ff), lambda i, s, te, tb, nu: (te[i], 0, 0)),
                  pl.BlockSpec((1, ff, tn), lambda i, s, te, tb, nu: (te[i], 0, nidx(i, s, nu))),
                  pl.BlockSpec((1, ff, tn), lambda i, s, te, tb, nu: (te[i], 0, nn + nidx(i, s, nu))),
                  pl.BlockSpec((1, 1, tn), lambda i, s, te, tb, nu: (te[i], 0, nidx(i, s, nu))),
                  pl.BlockSpec((1, 1, tn), lambda i, s, te, tb, nu: (te[i], 0, nn + nidx(i, s, nu)))],
        out_specs=pl.BlockSpec((tm, tn), lambda i, s, te, tb, nu: (i, jnp.maximum(s - nk, 0))),
        scratch_shapes=[pltpu.VMEM((nk, tm, tk), BF16), pltpu.VMEM((tm, ff), F32),
                        pltpu.VMEM((tm, ff), F32), pltpu.VMEM((tm, ff), BF16)],
    )
    return pl.pallas_call(
        functools.partial(_experts_kernel, nk=nk),
        grid_spec=grid_spec,
        out_shape=jax.ShapeDtypeStruct((p, half), U32),
        compiler_params=_params(("arbitrary", "arbitrary")),
        name="experts",
    )(tile_expert, tile_block, n_used, xs, gw, uw, gb, ub, dw, dw, db, db)


def _combine_kernel(pos_ref, nxt_ref, h_ref, w_ref, nw_ref, y_ref, o_ref, buf_ref, sem):
    i = pl.program_id(0)
    n = pl.num_programs(0)
    tc = h_ref.shape[0]
    half = buf_ref.shape[3]
    slot = i % 2

    def gather(idx_ref, sl):
        def issue(ti, carry):
            for kk in range(TOP_K):
                p = idx_ref[0, 0, kk * tc + ti]
                pltpu.make_async_copy(y_ref.at[pl.ds(p, 1)], buf_ref.at[sl, kk, pl.ds(ti, 1)], sem.at[sl]).start()
            return carry

        lax.fori_loop(0, tc, issue, 0)

    @pl.when(i == 0)
    def _():
        gather(pos_ref, slot)

    @pl.when(i + 1 < n)
    def _():
        gather(nxt_ref, 1 - slot)

    for kk in range(TOP_K):
        pltpu.make_async_copy(y_ref.at[pl.ds(0, tc)], buf_ref.at[slot, kk], sem.at[slot]).wait()

    w = w_ref[...]
    h = h_ref[...]
    lo = h[:, :half]
    hi = h[:, half:]
    for kk in range(TOP_K):
        u = buf_ref[slot, kk]
        wk = w[:, kk:kk + 1]
        lo = lo + wk * _unpack_lo(u)
        hi = hi + wk * _unpack_hi(u)
    ms = (jnp.sum(lo * lo, axis=-1, keepdims=True) + jnp.sum(hi * hi, axis=-1, keepdims=True)) / (2 * half)
    inv = lax.rsqrt(ms + EPS)
    nw = nw_ref[...]
    o_ref[:, :half] = lo * inv * nw[:, :half]
    o_ref[:, half:] = hi * inv * nw[:, half:]


def _combine(pos_tiles, h1, topw, nw, y, tc):
    t, d = h1.shape
    n = t // tc
    return pl.pallas_call(
        _combine_kernel,
        grid=(n,),
        in_specs=[pl.BlockSpec((1, 1, TOP_K * tc), lambda i: (i, 0, 0), memory_space=pltpu.SMEM),
                  pl.BlockSpec((1, 1, TOP_K * tc), lambda i: (jnp.minimum(i + 1, n - 1), 0, 0),
                               memory_space=pltpu.SMEM),
                  pl.BlockSpec((tc, d), lambda i: (i, 0)),
                  pl.BlockSpec((tc, LANES), lambda i: (i, 0)),
                  pl.BlockSpec((1, d), lambda i: (0, 0)),
                  pl.BlockSpec(memory_space=pl.ANY)],
        out_specs=pl.BlockSpec((tc, d), lambda i: (i, 0)),
        out_shape=jax.ShapeDtypeStruct((t, d), F32),
        scratch_shapes=[pltpu.VMEM((2, TOP_K, tc, d // 2), U32), pltpu.SemaphoreType.DMA((2,))],
        compiler_params=_params(("arbitrary",)),
        name="combine",
    )(pos_tiles, pos_tiles, h1, topw, nw, y)


def _tiles(t, seq, d):
    return dict(
        norm_tm=min(512, t),
        proj_tm=min(1024, t), proj_tn=min(1024, d // 2),
        rec_ts=min(512, seq),
        out_tm=min(512, t), out_tn=min(1024, d),
        router_tr=min(512, t),
        scatter_td=min(256, t),
        expert_tm=min(512, t), expert_tk=min(1024, d // 2), expert_tn=min(512, d // 2),
        combine_tc=min(256, t),
    )


def _pos_tiles(pos, tile):
    t = pos.shape[0]
    return pos.reshape(t // tile, tile, TOP_K).transpose(0, 2, 1).reshape(t // tile, 1, TOP_K * tile)


def kernel(x, attn_norm_w, in_proj_w, gla_gate_up_w, gla_gate_up_b, gla_out_norm_w, hgrn_lb_logits, hgrn_out_norm_w, out_proj_w, ffn_norm_w, router_w, router_b, expert_gate_w, expert_gate_b, expert_up_w, expert_up_b, expert_down_w, expert_down_b, final_norm_w):
    batch, seq, d = x.shape
    depth = attn_norm_w.shape[0]
    assert depth == 1 and d % 2048 == 0 and seq % CHUNK == 0
    t = batch * seq
    kd = d // 4
    gw = d // 2
    hw = d - gw
    ff = expert_gate_w.shape[-1]
    ts = _tiles(t, seq, d)

    lower = jnp.cumsum(jax.nn.softmax(hgrn_lb_logits.astype(F32), axis=0), axis=0)[0:1]

    w_in = in_proj_w[0]
    c_lr = 2 * kd + 2 * gw
    w_main = jnp.concatenate([w_in[:, :c_lr], w_in[:, c_lr + GLA_RANK:]], axis=1).astype(BF16)
    w_lr = jnp.pad(w_in[:, c_lr:c_lr + GLA_RANK], ((0, 0), (0, LANES - GLA_RANK))).astype(BF16)
    up_w = jnp.pad(gla_gate_up_w[0], ((0, LANES - GLA_RANK), (0, 0)))
    w_out = out_proj_w[0].astype(BF16)
    r_w = jnp.pad(router_w[0], ((0, 0), (0, LANES - N_EXPERTS)))
    r_b = jnp.pad(router_b, ((0, 0), (0, LANES - N_EXPERTS)))

    x2 = x.reshape(t, d)
    xn, g_gla = _norm_gate(x2, attn_norm_w, w_lr, up_w, gla_gate_up_b, ts["norm_tm"])
    proj = _in_proj(xn, w_main, ts["proj_tm"], ts["proj_tn"])
    o_a = _gla(proj, g_gla, gla_out_norm_w, batch, seq, ts["rec_ts"])
    o_b = _hgrn(proj, lower, hgrn_out_norm_w, batch, seq, ts["rec_ts"], hw, c_lr // hw)
    h1 = _out_proj(o_a, o_b, w_out, x2, ts["out_tm"], ts["out_tn"])

    topi, topw, rank, counts = _router(h1, ffn_norm_w, r_w, r_b, ts["router_tr"])

    tm = ts["expert_tm"]
    assert tm % ts["scatter_td"] == 0
    n_tiles = (t * TOP_K) // tm + N_EXPERTS
    cnt = counts[0, :N_EXPERTS].astype(I32)
    tiles_e = (cnt + tm - 1) // tm
    tile_end = jnp.cumsum(tiles_e)
    row_start = (tile_end - tiles_e) * tm
    n_used = tile_end[-1]
    pos = row_start[topi[:, :TOP_K]] + rank[:, :TOP_K]
    tile_id = jnp.arange(n_tiles, dtype=I32)
    tile_block = jnp.minimum(tile_id, n_used - 1)
    tile_expert = jnp.sum((tile_block[:, None] >= tile_end[None, :]).astype(I32), axis=1)

    pad_start = jnp.concatenate([row_start + cnt, (n_used * tm)[None]])
    pad_len = jnp.concatenate([tiles_e * tm - cnt, ((n_tiles - n_used) * tm)[None]])
    pad_spans = jnp.concatenate([pad_start, pad_len]).astype(I32)[None, None, :]
    xs = _dispatch(_pos_tiles(pos, ts["scatter_td"]), pad_spans, h1, ffn_norm_w, n_tiles * tm, ts["scatter_td"])
    y = _experts(tile_expert, tile_block, n_used.reshape(1), xs,
                 expert_gate_w[0], expert_gate_b[0][:, None, :], expert_up_w[0], expert_up_b[0][:, None, :],
                 expert_down_w[0], expert_down_b[0][:, None, :], tm, ts["expert_tk"], ts["expert_tn"])
    out = _combine(_pos_tiles(pos, ts["combine_tc"]), h1, topw, final_norm_w[None, :], y, ts["combine_tc"])
    return out.reshape(batch, seq, d)
```

```python
import functools

import jax
import jax.numpy as jnp
from jax import lax
from jax.experimental import pallas as pl
from jax.experimental.pallas import tpu as pltpu

CHUNK = 64
N_EXPERTS = 32
TOP_K = 4
GLA_HEADS = 4
GLA_RANK = 16
GLA_GATE_NORM = 16.0
HGRN_EXPAND = 128
SWIGLU_ALPHA = 1.702
SWIGLU_LIMIT = 7.0
EPS = 1e-5
LANES = 128
VMEM_LIMIT = 56 * 1024 * 1024

F32 = jnp.float32
BF16 = jnp.bfloat16
U32 = jnp.uint32
I32 = jnp.int32


def _params(semantics):
    return pltpu.CompilerParams(dimension_semantics=semantics, vmem_limit_bytes=VMEM_LIMIT)


def _dot(a, b):
    return jnp.dot(a, b, preferred_element_type=F32)


def _dot_nt(a, b):
    return lax.dot_general(a, b, (((1,), (1,)), ((), ())), preferred_element_type=F32)


def _dot_tn(a, b):
    return lax.dot_general(a, b, (((0,), (0,)), ((), ())), preferred_element_type=F32)


def _rmsnorm(x, w):
    return x * lax.rsqrt(jnp.mean(x * x, axis=-1, keepdims=True) + EPS) * w


def _pack_bf16_pairs(x):
    n = x.shape[1] // 2
    return _pack2(x[:, :n], x[:, n:])


def _pack2(lo, hi):
    return pltpu.pack_elementwise([lo, hi], packed_dtype=BF16)


def _unpack_lo(u):
    return pltpu.unpack_elementwise(u, index=0, packed_dtype=BF16, unpacked_dtype=F32)


def _unpack_hi(u):
    return pltpu.unpack_elementwise(u, index=1, packed_dtype=BF16, unpacked_dtype=F32)


def _norm_gate_kernel(x_ref, nw_ref, wlr_ref, upw_ref, upb_ref, xn_ref, g_ref):
    xn = _rmsnorm(x_ref[...], nw_ref[...]).astype(BF16)
    xn_ref[...] = xn
    lr = _dot(xn, wlr_ref[...])
    z = jnp.dot(lr, upw_ref[...], precision=lax.Precision.HIGHEST,
                preferred_element_type=F32) + upb_ref[...]
    log_sig = jnp.minimum(z, 0.0) - jnp.log1p(jnp.exp(-jnp.abs(z)))
    g_ref[...] = log_sig * (1.0 / GLA_GATE_NORM)


def _norm_gate(x2, nw, wlr, upw, upb, tm):
    t, d = x2.shape
    kd = upw.shape[1]
    return pl.pallas_call(
        _norm_gate_kernel,
        grid=(t // tm,),
        in_specs=[pl.BlockSpec((tm, d), lambda i: (i, 0)),
                  pl.BlockSpec((1, d), lambda i: (0, 0)),
                  pl.BlockSpec((d, LANES), lambda i: (0, 0)),
                  pl.BlockSpec((LANES, kd), lambda i: (0, 0)),
                  pl.BlockSpec((1, kd), lambda i: (0, 0))],
        out_specs=[pl.BlockSpec((tm, d), lambda i: (i, 0)),
                   pl.BlockSpec((tm, kd), lambda i: (i, 0))],
        out_shape=[jax.ShapeDtypeStruct((t, d), BF16), jax.ShapeDtypeStruct((t, kd), F32)],
        compiler_params=_params(("parallel",)),
        name="norm_gate",
    )(x2, nw, wlr, upw, upb)


def _matmul_kernel(a_ref, b_ref, o_ref):
    o_ref[...] = _dot(a_ref[...], b_ref[...]).astype(o_ref.dtype)


def _in_proj(xn, w, tm, tn):
    t, d = xn.shape
    n = w.shape[1]
    return pl.pallas_call(
        _matmul_kernel,
        grid=(n // tn, t // tm),
        in_specs=[pl.BlockSpec((tm, d), lambda j, i: (i, 0)),
                  pl.BlockSpec((d, tn), lambda j, i: (0, j))],
        out_specs=pl.BlockSpec((tm, tn), lambda j, i: (i, j)),
        out_shape=jax.ShapeDtypeStruct((t, n), BF16),
        compiler_params=_params(("parallel", "parallel")),
        name="in_proj",
    )(xn, w)


def _decay_levels(w):
    c = w.shape[0]
    row = lax.broadcasted_iota(I32, (c, 1), 0)
    pre, suf, tot = w, None, w
    levels = []
    s = 1
    while s < 8:
        levels.append((pre, suf))
        odd = (row & s) != 0
        prev_tot = pltpu.roll(tot, s, 0)
        next_tot = pltpu.roll(tot, c - s, 0)
        pre = pre * jnp.where(odd, prev_tot, 1.0)
        grow = jnp.where(odd, 1.0, next_tot)
        suf = grow if suf is None else suf * grow
        tot = tot * jnp.where(odd, prev_tot, next_tot)
        s *= 2
    groups = c // 8
    pre_g = [pre[8 * r:8 * r + 8] for r in range(groups)]
    suf_g = [suf[8 * r:8 * r + 8] for r in range(groups)]
    tot_g = [tot[8 * r:8 * r + 8] for r in range(groups)]
    m = 1
    while 8 * m < c:
        levels.append((jnp.concatenate(pre_g, axis=0), jnp.concatenate(suf_g, axis=0)))
        new_tot = []
        for r in range(groups):
            if (r // m) & 1:
                pre_g[r] = pre_g[r] * tot_g[r - m]
                new_tot.append(tot_g[r] * tot_g[r - m])
            else:
                suf_g[r] = suf_g[r] * tot_g[r + m]
                new_tot.append(tot_g[r] * tot_g[r + m])
        tot_g = new_tot
        m *= 2
    return levels, jnp.concatenate(pre_g, axis=0), jnp.concatenate(suf_g, axis=0), tot_g[0][0:1]


def _score_masks(c):
    row = lax.broadcasted_iota(I32, (c, c), 0)
    col = lax.broadcasted_iota(I32, (c, c), 1)
    masks = [row == col]
    s, sh = 1, 0
    while s < c:
        masks.append((((row ^ col) >> sh) == 1) & ((row & s) != 0))
        s *= 2
        sh += 1
    return masks


def _recurrence_chunk(q, k, w, v, r, nw, st_ref, o_ref, rows, n_heads, dk, dv, masks):
    levels, pre_c, suf_c, dec = _decay_levels(w)
    kb = k.astype(BF16)
    ops = [(q.astype(BF16), kb)]
    for pre, suf in levels:
        ops.append(((q * pre).astype(BF16), kb if suf is None else (k * suf).astype(BF16)))
    qd = (q * pre_c).astype(BF16)
    kd = (k * suf_c).astype(BF16)
    for h in range(n_heads):
        sk = slice(h * dk, (h + 1) * dk)
        sv = slice(h * dv, (h + 1) * dv)
        a = None
        for (qs, ks), m in zip(ops, masks):
            part = jnp.where(m, _dot_nt(qs[:, sk], ks[:, sk]), 0.0)
            a = part if a is None else a + part
        st = st_ref[h]
        vh = v[:, sv]
        o = _dot_nt(qd[:, sk], st.astype(BF16)) + _dot(a.astype(BF16), vh)
        st_ref[h] = st * dec[:, sk] + _dot_tn(vh, kd[:, sk])
        y = o * lax.rsqrt(jnp.mean(o * o, axis=-1, keepdims=True) + EPS) * nw
        rh = r[:, sv]
        o_ref[rows, sv] = (y * (rh * jax.nn.sigmoid(rh))).astype(o_ref.dtype)


def _gla_kernel(q_ref, k_ref, v_ref, r_ref, g_ref, nw_ref, o_ref, st_ref, *, scale, n_chunks):
    @pl.when(pl.program_id(1) == 0)
    def _():
        st_ref[...] = jnp.zeros_like(st_ref)

    masks = _score_masks(CHUNK)
    n_heads, dv, dk = st_ref.shape
    nw = nw_ref[...]

    def body(ci, carry):
        rows = pl.ds(pl.multiple_of(ci * CHUNK, CHUNK), CHUNK)
        q = q_ref[rows, :].astype(F32) * scale
        k = k_ref[rows, :].astype(F32)
        _recurrence_chunk(q, k, jnp.exp(g_ref[rows, :]), v_ref[rows, :], r_ref[rows, :].astype(F32), nw,
                          st_ref, o_ref, rows, n_heads, dk, dv, masks)
        return carry

    lax.fori_loop(0, n_chunks, body, 0, unroll=2)


def _gla(proj, g, nw, batch, seq, ts):
    t, _ = proj.shape
    kd = g.shape[1]
    gw = 2 * kd
    hk = kd // GLA_HEADS
    hv = gw // GLA_HEADS
    ns = seq // ts
    row = lambda b, s: b * ns + s
    return pl.pallas_call(
        functools.partial(_gla_kernel, scale=hk ** -0.5, n_chunks=ts // CHUNK),
        grid=(batch, ns),
        in_specs=[pl.BlockSpec((ts, kd), lambda b, s: (row(b, s), 0)),
                  pl.BlockSpec((ts, kd), lambda b, s: (row(b, s), 1)),
                  pl.BlockSpec((ts, gw), lambda b, s: (row(b, s), 1)),
                  pl.BlockSpec((ts, gw), lambda b, s: (row(b, s), 2)),
                  pl.BlockSpec((ts, kd), lambda b, s: (row(b, s), 0)),
                  pl.BlockSpec((1, hv), lambda b, s: (0, 0))],
        out_specs=pl.BlockSpec((ts, gw), lambda b, s: (row(b, s), 0)),
        out_shape=jax.ShapeDtypeStruct((t, gw), BF16),
        scratch_shapes=[pltpu.VMEM((GLA_HEADS, hv, hk), F32)],
        compiler_params=_params(("parallel", "arbitrary")),
        name="gla",
    )(proj, proj, proj, proj, g, nw)


def _hgrn_kernel(q_ref, f_ref, i_ref, r_ref, lb_ref, nw_ref, o_ref, st_ref, *, n_chunks):
    @pl.when(pl.program_id(1) == 0)
    def _():
        st_ref[...] = jnp.zeros_like(st_ref)

    masks = _score_masks(CHUNK)
    n_heads = st_ref.shape[0]
    lb = lb_ref[...]
    nw = nw_ref[...]

    def body(ci, carry):
        rows = pl.ds(pl.multiple_of(ci * CHUNK, CHUNK), CHUNK)
        zf = f_ref[rows, :].astype(F32)
        forget = lb + (1.0 - lb) * jax.nn.sigmoid(zf)
        k = (1.0 - lb) * jax.nn.sigmoid(-zf)
        qz = q_ref[rows, :].astype(F32)
        q = qz * jax.nn.sigmoid(qz)
        _recurrence_chunk(q, k, forget, i_ref[rows, :], r_ref[rows, :].astype(F32), nw,
                          st_ref, o_ref, rows, n_heads, HGRN_EXPAND, HGRN_EXPAND, masks)
        return carry

    lax.fori_loop(0, n_chunks, body, 0, unroll=2)


def _hgrn(proj, lb, nw, batch, seq, ts, hw, col0):
    t, _ = proj.shape
    ns = seq // ts
    n_heads = hw // HGRN_EXPAND
    row = lambda b, s: b * ns + s
    return pl.pallas_call(
        functools.partial(_hgrn_kernel, n_chunks=ts // CHUNK),
        grid=(batch, ns),
        in_specs=[pl.BlockSpec((ts, hw), lambda b, s: (row(b, s), col0)),
                  pl.BlockSpec((ts, hw), lambda b, s: (row(b, s), col0 + 1)),
                  pl.BlockSpec((ts, hw), lambda b, s: (row(b, s), col0 + 2)),
                  pl.BlockSpec((ts, hw), lambda b, s: (row(b, s), col0 + 3)),
                  pl.BlockSpec((1, hw), lambda b, s: (0, 0)),
                  pl.BlockSpec((1, HGRN_EXPAND), lambda b, s: (0, 0))],
        out_specs=pl.BlockSpec((ts, hw), lambda b, s: (row(b, s), 0)),
        out_shape=jax.ShapeDtypeStruct((t, hw), BF16),
        scratch_shapes=[pltpu.VMEM((n_heads, HGRN_EXPAND, HGRN_EXPAND), F32)],
        compiler_params=_params(("parallel", "arbitrary")),
        name="hgrn",
    )(proj, proj, proj, proj, lb, nw)


def _out_proj_kernel(oa_ref, ob_ref, wa_ref, wb_ref, x_ref, h_ref):
    h_ref[...] = x_ref[...] + _dot(oa_ref[...], wa_ref[...]) + _dot(ob_ref[...], wb_ref[...])


def _out_proj(oa, ob, w, x2, tm, tn):
    t, d = x2.shape
    ka, kb = oa.shape[1], ob.shape[1]
    assert ka == kb and w.shape[0] == ka + kb
    return pl.pallas_call(
        _out_proj_kernel,
        grid=(d // tn, t // tm),
        in_specs=[pl.BlockSpec((tm, ka), lambda j, i: (i, 0)),
                  pl.BlockSpec((tm, kb), lambda j, i: (i, 0)),
                  pl.BlockSpec((ka, tn), lambda j, i: (0, j)),
                  pl.BlockSpec((kb, tn), lambda j, i: (1, j)),
                  pl.BlockSpec((tm, tn), lambda j, i: (i, j))],
        out_specs=pl.BlockSpec((tm, tn), lambda j, i: (i, j)),
        out_shape=jax.ShapeDtypeStruct((t, d), F32),
        compiler_params=_params(("parallel", "parallel")),
        name="out_proj",
    )(oa, ob, w, w, x2)


def _router_kernel(h_ref, nw_ref, rw_ref, rb_ref, topi_ref, topw_ref, rank_ref, cnt_ref, run_ref):
    step = pl.program_id(0)

    @pl.when(step == 0)
    def _():
        run_ref[...] = jnp.zeros_like(run_ref)

    tr = h_ref.shape[0]
    xn = _rmsnorm(h_ref[...], nw_ref[...])
    logits = jnp.dot(xn, rw_ref[...], precision=lax.Precision.HIGHEST,
                     preferred_element_type=F32) + rb_ref[...]
    lane = lax.broadcasted_iota(I32, (tr, LANES), 1)
    neg = jnp.float32(-jnp.inf)
    cur = jnp.where(lane < N_EXPERTS, logits, neg)
    vals, sels = [], []
    topi = jnp.zeros((tr, LANES), I32)
    for kk in range(TOP_K):
        m = jnp.max(cur, axis=-1, keepdims=True)
        idx = jnp.min(jnp.where(cur == m, lane, LANES), axis=-1, keepdims=True)
        sel = lane == idx
        vals.append(m)
        sels.append(sel)
        topi = jnp.where(lane == kk, idx, topi)
        cur = jnp.where(sel, neg, cur)
    exps = [jnp.exp(v - vals[0]) for v in vals]
    denom = exps[0] + exps[1] + exps[2] + exps[3]
    topw = jnp.zeros((tr, LANES), F32)
    hot = jnp.zeros((tr, LANES), F32)
    for kk in range(TOP_K):
        topw = jnp.where(lane == kk, exps[kk] / denom, topw)
        hot = hot + sels[kk].astype(F32)
    r_i = lax.broadcasted_iota(I32, (tr, tr), 0)
    c_i = lax.broadcasted_iota(I32, (tr, tr), 1)
    lower = (c_i < r_i).astype(BF16)
    before = _dot(lower, hot.astype(BF16)) + run_ref[...]
    rank = jnp.zeros((tr, LANES), F32)
    for kk in range(TOP_K):
        rk = jnp.sum(jnp.where(sels[kk], before, 0.0), axis=-1, keepdims=True)
        rank = jnp.where(lane == kk, rk, rank)
    topi_ref[...] = topi
    topw_ref[...] = topw
    rank_ref[...] = rank.astype(I32)
    run_ref[...] = run_ref[...] + jnp.sum(hot, axis=0, keepdims=True)
    cnt_ref[...] = run_ref[...]


def _router(h1, nw, rw, rb, tr):
    t, d = h1.shape
    tile = lambda i: (i, 0)
    fixed = lambda i: (0, 0)
    return pl.pallas_call(
        _router_kernel,
        grid=(t // tr,),
        in_specs=[pl.BlockSpec((tr, d), tile),
                  pl.BlockSpec((1, d), fixed),
                  pl.BlockSpec((d, LANES), fixed),
                  pl.BlockSpec((1, LANES), fixed)],
        out_specs=[pl.BlockSpec((tr, LANES), tile),
                   pl.BlockSpec((tr, LANES), tile),
                   pl.BlockSpec((tr, LANES), tile),
                   pl.BlockSpec((1, LANES), fixed)],
        out_shape=[jax.ShapeDtypeStruct((t, LANES), I32),
                   jax.ShapeDtypeStruct((t, LANES), F32),
                   jax.ShapeDtypeStruct((t, LANES), I32),
                   jax.ShapeDtypeStruct((1, LANES), F32)],
        scratch_shapes=[pltpu.VMEM((1, LANES), F32)],
        compiler_params=_params(("arbitrary",)),
        name="router",
    )(h1, nw, rw, rb)


def _zero_rows(pad_ref, xs_ref, zero_ref, sem, n_spans, wait):
    zr = zero_ref.shape[0]

    def run(copy):
        copy.wait() if wait else copy.start()

    def span(e, carry):
        start = pad_ref[0, 0, e]

        def one(j, c2):
            run(pltpu.make_async_copy(zero_ref.at[pl.ds(0, 1)], xs_ref.at[pl.ds(start + j, 1)], sem))
            return c2

        lax.fori_loop(0, pad_ref[0, 0, n_spans + e], one, 0)
        return carry

    lax.fori_loop(0, n_spans - 1, span, 0)
    tail = pad_ref[0, 0, n_spans - 1]

    def big(j, carry):
        run(pltpu.make_async_copy(zero_ref, xs_ref.at[pl.ds(pl.multiple_of(tail + j * zr, zr), zr)], sem))
        return carry

    lax.fori_loop(0, pad_ref[0, 0, 2 * n_spans - 1] // zr, big, 0)


def _dispatch_kernel(pos_ref, pad_ref, h_ref, nw_ref, xs_ref, buf_ref, zero_ref, sem, zsem, *, n_spans):
    i = pl.program_id(0)
    n = pl.num_programs(0)
    td = h_ref.shape[0]
    slot = i % 2

    def wait_rows(sl):
        for _ in range(TOP_K):
            pltpu.make_async_copy(buf_ref.at[sl], xs_ref.at[pl.ds(0, td)], sem.at[sl]).wait()

    @pl.when(i == 0)
    def _():
        zeros = jnp.zeros(zero_ref.shape, F32)
        zero_ref[...] = _pack2(zeros, zeros)
        _zero_rows(pad_ref, xs_ref, zero_ref, zsem, n_spans, wait=False)

    @pl.when(i >= 2)
    def _():
        wait_rows(slot)

    buf_ref[slot] = _pack_bf16_pairs(_rmsnorm(h_ref[...], nw_ref[...]))

    def issue(ti, carry):
        for kk in range(TOP_K):
            p = pos_ref[0, 0, kk * td + ti]
            pltpu.make_async_copy(buf_ref.at[slot, pl.ds(ti, 1)], xs_ref.at[pl.ds(p, 1)], sem.at[slot]).start()
        return carry

    lax.fori_loop(0, td, issue, 0)

    @pl.when(i == 0)
    def _():
        _zero_rows(pad_ref, xs_ref, zero_ref, zsem, n_spans, wait=True)

    @pl.when(i == n - 1)
    def _():
        wait_rows(slot)

    @pl.when((i == n - 1) & (n > 1))
    def _():
        wait_rows(1 - slot)


def _dispatch(pos_tiles, pad_spans, h1, nw, n_rows, td):
    t, d = h1.shape
    n_spans = pad_spans.shape[2] // 2
    return pl.pallas_call(
        functools.partial(_dispatch_kernel, n_spans=n_spans),
        grid=(t // td,),
        in_specs=[pl.BlockSpec((1, 1, TOP_K * td), lambda i: (i, 0, 0), memory_space=pltpu.SMEM),
                  pl.BlockSpec((1, 1, 2 * n_spans), lambda i: (0, 0, 0), memory_space=pltpu.SMEM),
                  pl.BlockSpec((td, d), lambda i: (i, 0)),
                  pl.BlockSpec((1, d), lambda i: (0, 0))],
        out_specs=pl.BlockSpec(memory_space=pl.ANY),
        out_shape=jax.ShapeDtypeStruct((n_rows, d // 2), U32),
        scratch_shapes=[pltpu.VMEM((2, td, d // 2), U32), pltpu.VMEM((td, d // 2), U32),
                        pltpu.SemaphoreType.DMA((2,)), pltpu.SemaphoreType.DMA],
        compiler_params=_params(("arbitrary",)),
        name="dispatch",
    )(pos_tiles, pad_spans, h1, nw)


def _experts_kernel(te_ref, tb_ref, nu_ref, x_ref, gw_ref, uw_ref, gb_ref, ub_ref,
                    dlo_ref, dhi_ref, blo_ref, bhi_ref, y_ref, xb_ref, hid_ref, *, nf):
    del te_ref, tb_ref
    i = pl.program_id(0)
    s = pl.program_id(1)
    half = x_ref.shape[1]
    tf = gw_ref.shape[2]
    active = i < nu_ref[0]

    @pl.when(active & (s == 0))
    def _():
        u = x_ref[...]
        xb_ref[:, :half] = _unpack_lo(u).astype(BF16)
        xb_ref[:, half:] = _unpack_hi(u).astype(BF16)

    @pl.when(active & (s < nf))
    def _():
        xb = xb_ref[...]
        g = jnp.minimum(_dot(xb, gw_ref[0]) + gb_ref[0], SWIGLU_LIMIT)
        u = jnp.clip(_dot(xb, uw_ref[0]) + ub_ref[0], -SWIGLU_LIMIT, SWIGLU_LIMIT)
        hid_ref[s] = (g * jax.nn.sigmoid(SWIGLU_ALPHA * g) * (u + 1.0)).astype(BF16)

    @pl.when(active & (s >= nf))
    def _():
        lo = blo_ref[0]
        hi = bhi_ref[0]
        for f in range(nf):
            hf = hid_ref[f]
            lo = lo + _dot(hf, dlo_ref[0, f * tf:(f + 1) * tf, :])
            hi = hi + _dot(hf, dhi_ref[0, f * tf:(f + 1) * tf, :])
        y_ref[...] = _pack2(lo, hi)

    @pl.when(jnp.logical_not(active) & (s >= nf))
    def _():
        zeros = jnp.zeros(y_ref.shape, F32)
        y_ref[...] = _pack2(zeros, zeros)


def _experts(tile_expert, tile_block, n_used, xs, gw, gb, uw, ub, dw, db, tm, tf, tn):
    p, half = xs.shape
    d = 2 * half
    ff = gw.shape[2]
    nf = ff // tf
    nn = half // tn
    n_tiles = p // tm

    def fidx(i, s, nu):
        return jnp.where(i < nu[0], jnp.minimum(s, nf - 1), nf - 1)

    def nidx(i, s, nu):
        return jnp.where(i < nu[0], jnp.maximum(s - nf, 0), nn - 1)

    grid_spec = pltpu.PrefetchScalarGridSpec(
        num_scalar_prefetch=3,
        grid=(n_tiles, nf + nn),
        in_specs=[pl.BlockSpec((tm, half), lambda i, s, te, tb, nu: (tb[i], 0)),
                  pl.BlockSpec((1, d, tf), lambda i, s, te, tb, nu: (te[i], 0, fidx(i, s, nu))),
                  pl.BlockSpec((1, d, tf), lambda i, s, te, tb, nu: (te[i], 0, fidx(i, s, nu))),
                  pl.BlockSpec((1, 1, tf), lambda i, s, te, tb, nu: (te[i], 0, fidx(i, s, nu))),
                  pl.BlockSpec((1, 1, tf), lambda i, s, te, tb, nu: (te[i], 0, fidx(i, s, nu))),
                  pl.BlockSpec((1, ff, tn), lambda i, s, te, tb, nu: (te[i], 0, nidx(i, s, nu))),
                  pl.BlockSpec((1, ff, tn), lambda i, s, te, tb, nu: (te[i], 0, nn + nidx(i, s, nu))),
                  pl.BlockSpec((1, 1, tn), lambda i, s, te, tb, nu: (te[i], 0, nidx(i, s, nu))),
                  pl.BlockSpec((1, 1, tn), lambda i, s, te, tb, nu: (te[i], 0, nn + nidx(i, s, nu)))],
        out_specs=pl.BlockSpec((tm, tn), lambda i, s, te, tb, nu: (i, jnp.maximum(s - nf, 0))),
        scratch_shapes=[pltpu.VMEM((tm, d), BF16), pltpu.VMEM((nf, tm, tf), BF16)],
    )
    return pl.pallas_call(
        functools.partial(_experts_kernel, nf=nf),
        grid_spec=grid_spec,
        out_shape=jax.ShapeDtypeStruct((p, half), U32),
        compiler_params=_params(("arbitrary", "arbitrary")),
        name="experts",
    )(tile_expert, tile_block, n_used, xs, gw, uw, gb, ub, dw, dw, db, db)


def _combine_kernel(pos_ref, nxt_ref, h_ref, w_ref, nw_ref, y_ref, o_ref, buf_ref, sem):
    i = pl.program_id(0)
    n = pl.num_programs(0)
    tc = h_ref.shape[0]
    half = buf_ref.shape[3]
    slot = i % 2

    def gather(idx_ref, sl):
        def issue(ti, carry):
            for kk in range(TOP_K):
                p = idx_ref[0, 0, kk * tc + ti]
                pltpu.make_async_copy(y_ref.at[pl.ds(p, 1)], buf_ref.at[sl, kk, pl.ds(ti, 1)], sem.at[sl]).start()
            return carry

        lax.fori_loop(0, tc, issue, 0)

    @pl.when(i == 0)
    def _():
        gather(pos_ref, slot)

    @pl.when(i + 1 < n)
    def _():
        gather(nxt_ref, 1 - slot)

    for kk in range(TOP_K):
        pltpu.make_async_copy(y_ref.at[pl.ds(0, tc)], buf_ref.at[slot, kk], sem.at[slot]).wait()

    w = w_ref[...]
    h = h_ref[...]
    lo = h[:, :half]
    hi = h[:, half:]
    for kk in range(TOP_K):
        u = buf_ref[slot, kk]
        wk = w[:, kk:kk + 1]
        lo = lo + wk * _unpack_lo(u)
        hi = hi + wk * _unpack_hi(u)
    ms = (jnp.sum(lo * lo, axis=-1, keepdims=True) + jnp.sum(hi * hi, axis=-1, keepdims=True)) / (2 * half)
    inv = lax.rsqrt(ms + EPS)
    nw = nw_ref[...]
    o_ref[:, :half] = lo * inv * nw[:, :half]
    o_ref[:, half:] = hi * inv * nw[:, half:]


def _combine(pos_tiles, h1, topw, nw, y, tc):
    t, d = h1.shape
    n = t // tc
    return pl.pallas_call(
        _combine_kernel,
        grid=(n,),
        in_specs=[pl.BlockSpec((1, 1, TOP_K * tc), lambda i: (i, 0, 0), memory_space=pltpu.SMEM),
                  pl.BlockSpec((1, 1, TOP_K * tc), lambda i: (jnp.minimum(i + 1, n - 1), 0, 0),
                               memory_space=pltpu.SMEM),
                  pl.BlockSpec((tc, d), lambda i: (i, 0)),
                  pl.BlockSpec((tc, LANES), lambda i: (i, 0)),
                  pl.BlockSpec((1, d), lambda i: (0, 0)),
                  pl.BlockSpec(memory_space=pl.ANY)],
        out_specs=pl.BlockSpec((tc, d), lambda i: (i, 0)),
        out_shape=jax.ShapeDtypeStruct((t, d), F32),
        scratch_shapes=[pltpu.VMEM((2, TOP_K, tc, d // 2), U32), pltpu.SemaphoreType.DMA((2,))],
        compiler_params=_params(("arbitrary",)),
        name="combine",
    )(pos_tiles, pos_tiles, h1, topw, nw, y)


def _tiles(t, seq, d):
    return dict(
        norm_tm=min(512, t),
        proj_tm=min(1024, t), proj_tn=min(1024, d // 2),
        rec_ts=min(512, seq),
        out_tm=min(512, t), out_tn=min(1024, d),
        router_tr=min(512, t),
        scatter_td=min(256, t),
        expert_tm=min(512, t), expert_tf=min(512, d // 4), expert_tn=min(1024, d // 2),
        combine_tc=min(256, t),
    )


def _pos_tiles(pos, tile):
    t = pos.shape[0]
    return pos.reshape(t // tile, tile, TOP_K).transpose(0, 2, 1).reshape(t // tile, 1, TOP_K * tile)


def kernel(x, attn_norm_w, in_proj_w, gla_gate_up_w, gla_gate_up_b, gla_out_norm_w, hgrn_lb_logits, hgrn_out_norm_w, out_proj_w, ffn_norm_w, router_w, router_b, expert_gate_w, expert_gate_b, expert_up_w, expert_up_b, expert_down_w, expert_down_b, final_norm_w):
    batch, seq, d = x.shape
    depth = attn_norm_w.shape[0]
    assert depth == 1 and d % 2048 == 0 and seq % CHUNK == 0
    t = batch * seq
    kd = d // 4
    gw = d // 2
    hw = d - gw
    ff = expert_gate_w.shape[-1]
    ts = _tiles(t, seq, d)

    lower = jnp.cumsum(jax.nn.softmax(hgrn_lb_logits.astype(F32), axis=0), axis=0)[0:1]

    w_in = in_proj_w[0]
    c_lr = 2 * kd + 2 * gw
    w_main = jnp.concatenate([w_in[:, :c_lr].astype(BF16), w_in[:, c_lr + GLA_RANK:].astype(BF16)], axis=1)
    w_lr = jnp.pad(w_in[:, c_lr:c_lr + GLA_RANK], ((0, 0), (0, LANES - GLA_RANK))).astype(BF16)
    up_w = jnp.pad(gla_gate_up_w[0], ((0, LANES - GLA_RANK), (0, 0)))
    w_out = out_proj_w[0].astype(BF16)
    r_w = jnp.pad(router_w[0], ((0, 0), (0, LANES - N_EXPERTS)))
    r_b = jnp.pad(router_b, ((0, 0), (0, LANES - N_EXPERTS)))

    x2 = x.reshape(t, d)
    xn, g_gla = _norm_gate(x2, attn_norm_w, w_lr, up_w, gla_gate_up_b, ts["norm_tm"])
    proj = _in_proj(xn, w_main, ts["proj_tm"], ts["proj_tn"])
    o_a = _gla(proj, g_gla, gla_out_norm_w, batch, seq, ts["rec_ts"])
    o_b = _hgrn(proj, lower, hgrn_out_norm_w, batch, seq, ts["rec_ts"], hw, c_lr // hw)
    h1 = _out_proj(o_a, o_b, w_out, x2, ts["out_tm"], ts["out_tn"])

    topi, topw, rank, counts = _router(h1, ffn_norm_w, r_w, r_b, ts["router_tr"])

    tm = ts["expert_tm"]
    assert tm % ts["scatter_td"] == 0
    n_tiles = (t * TOP_K) // tm + N_EXPERTS
    cnt = counts[0, :N_EXPERTS].astype(I32)
    tiles_e = (cnt + tm - 1) // tm
    tile_end = jnp.cumsum(tiles_e)
    row_start = (tile_end - tiles_e) * tm
    n_used = tile_end[-1]
    pos = row_start[topi[:, :TOP_K]] + rank[:, :TOP_K]
    tile_id = jnp.arange(n_tiles, dtype=I32)
    tile_block = jnp.minimum(tile_id, n_used - 1)
    tile_expert = jnp.sum((tile_block[:, None] >= tile_end[None, :]).astype(I32), axis=1)

    pad_start = jnp.concatenate([row_start + cnt, (n_used * tm)[None]])
    pad_len = jnp.concatenate([tiles_e * tm - cnt, ((n_tiles - n_used) * tm)[None]])
    pad_spans = jnp.concatenate([pad_start, pad_len]).astype(I32)[None, None, :]
    xs = _dispatch(_pos_tiles(pos, ts["scatter_td"]), pad_spans, h1, ffn_norm_w, n_tiles * tm, ts["scatter_td"])
    y = _experts(tile_expert, tile_block, n_used.reshape(1), xs,
                 expert_gate_w[0].astype(BF16), expert_gate_b[0][:, None, :],
                 expert_up_w[0].astype(BF16), expert_up_b[0][:, None, :],
                 expert_down_w[0].astype(BF16), expert_down_b[0][:, None, :], tm, ts["expert_tf"], ts["expert_tn"])
    out = _combine(_pos_tiles(pos, ts["combine_tc"]), h1, topw, final_norm_w[None, :], y, ts["combine_tc"])
    return out.reshape(batch, seq, d)
```

```python
import functools

import jax
import jax.numpy as jnp
from jax import lax
from jax.experimental import pallas as pl
from jax.experimental.pallas import tpu as pltpu

CHUNK = 64
N_EXPERTS = 32
TOP_K = 4
GLA_HEADS = 4
GLA_RANK = 16
GLA_GATE_NORM = 16.0
HGRN_EXPAND = 128
SWIGLU_ALPHA = 1.702
SWIGLU_LIMIT = 7.0
EPS = 1e-5
LANES = 128
VMEM_LIMIT = 56 * 1024 * 1024

F32 = jnp.float32
BF16 = jnp.bfloat16
U32 = jnp.uint32
I32 = jnp.int32


def _params(semantics):
    return pltpu.CompilerParams(dimension_semantics=semantics, vmem_limit_bytes=VMEM_LIMIT)


def _dot(a, b):
    return jnp.dot(a, b, preferred_element_type=F32)


def _dot_nt(a, b):
    return lax.dot_general(a, b, (((1,), (1,)), ((), ())), preferred_element_type=F32)


def _dot_tn(a, b):
    return lax.dot_general(a, b, (((0,), (0,)), ((), ())), preferred_element_type=F32)


def _rmsnorm(x, w):
    return x * lax.rsqrt(jnp.mean(x * x, axis=-1, keepdims=True) + EPS) * w


def _pack_bf16_pairs(x):
    n = x.shape[1] // 2
    return _pack2(x[:, :n], x[:, n:])


def _pack2(lo, hi):
    return pltpu.pack_elementwise([lo, hi], packed_dtype=BF16)


def _unpack_lo(u):
    return pltpu.unpack_elementwise(u, index=0, packed_dtype=BF16, unpacked_dtype=F32)


def _unpack_hi(u):
    return pltpu.unpack_elementwise(u, index=1, packed_dtype=BF16, unpacked_dtype=F32)


def _norm_gate_kernel(x_ref, nw_ref, wlr_ref, upw_ref, upb_ref, xn_ref, g_ref):
    xn = _rmsnorm(x_ref[...], nw_ref[...]).astype(BF16)
    xn_ref[...] = xn
    lr = _dot(xn, wlr_ref[...])
    z = jnp.dot(lr, upw_ref[...], precision=lax.Precision.HIGHEST,
                preferred_element_type=F32) + upb_ref[...]
    log_sig = jnp.minimum(z, 0.0) - jnp.log1p(jnp.exp(-jnp.abs(z)))
    g_ref[...] = log_sig * (1.0 / GLA_GATE_NORM)


def _norm_gate(x2, nw, wlr, upw, upb, tm):
    t, d = x2.shape
    kd = upw.shape[1]
    return pl.pallas_call(
        _norm_gate_kernel,
        grid=(t // tm,),
        in_specs=[pl.BlockSpec((tm, d), lambda i: (i, 0)),
                  pl.BlockSpec((1, d), lambda i: (0, 0)),
                  pl.BlockSpec((d, LANES), lambda i: (0, 0)),
                  pl.BlockSpec((LANES, kd), lambda i: (0, 0)),
                  pl.BlockSpec((1, kd), lambda i: (0, 0))],
        out_specs=[pl.BlockSpec((tm, d), lambda i: (i, 0)),
                   pl.BlockSpec((tm, kd), lambda i: (i, 0))],
        out_shape=[jax.ShapeDtypeStruct((t, d), BF16), jax.ShapeDtypeStruct((t, kd), F32)],
        compiler_params=_params(("parallel",)),
        name="norm_gate",
    )(x2, nw, wlr, upw, upb)


def _matmul_kernel(a_ref, b_ref, o_ref):
    o_ref[...] = _dot(a_ref[...], b_ref[...]).astype(o_ref.dtype)


def _in_proj(xn, w, tm, tn):
    t, d = xn.shape
    n = w.shape[1]
    return pl.pallas_call(
        _matmul_kernel,
        grid=(n // tn, t // tm),
        in_specs=[pl.BlockSpec((tm, d), lambda j, i: (i, 0)),
                  pl.BlockSpec((d, tn), lambda j, i: (0, j))],
        out_specs=pl.BlockSpec((tm, tn), lambda j, i: (i, j)),
        out_shape=jax.ShapeDtypeStruct((t, n), BF16),
        compiler_params=_params(("parallel", "parallel")),
        name="in_proj",
    )(xn, w)


def _decay_levels(w):
    c = w.shape[0]
    row = lax.broadcasted_iota(I32, (c, 1), 0)
    pre, suf, tot = w, None, w
    levels = []
    s = 1
    while s < 8:
        levels.append((pre, suf))
        odd = (row & s) != 0
        prev_tot = pltpu.roll(tot, s, 0)
        next_tot = pltpu.roll(tot, c - s, 0)
        pre = pre * jnp.where(odd, prev_tot, 1.0)
        grow = jnp.where(odd, 1.0, next_tot)
        suf = grow if suf is None else suf * grow
        tot = tot * jnp.where(odd, prev_tot, next_tot)
        s *= 2
    groups = c // 8
    pre_g = [pre[8 * r:8 * r + 8] for r in range(groups)]
    suf_g = [suf[8 * r:8 * r + 8] for r in range(groups)]
    tot_g = [tot[8 * r:8 * r + 8] for r in range(groups)]
    m = 1
    while 8 * m < c:
        levels.append((jnp.concatenate(pre_g, axis=0), jnp.concatenate(suf_g, axis=0)))
        new_tot = []
        for r in range(groups):
            if (r // m) & 1:
                pre_g[r] = pre_g[r] * tot_g[r - m]
                new_tot.append(tot_g[r] * tot_g[r - m])
            else:
                suf_g[r] = suf_g[r] * tot_g[r + m]
                new_tot.append(tot_g[r] * tot_g[r + m])
        tot_g = new_tot
        m *= 2
    return levels, jnp.concatenate(pre_g, axis=0), jnp.concatenate(suf_g, axis=0), tot_g[0][0:1]


def _score_masks(c):
    row = lax.broadcasted_iota(I32, (c, c), 0)
    col = lax.broadcasted_iota(I32, (c, c), 1)
    masks = [row == col]
    s, sh = 1, 0
    while s < c:
        masks.append((((row ^ col) >> sh) == 1) & ((row & s) != 0))
        s *= 2
        sh += 1
    return masks


def _recurrence_chunk(q, k, w, v, r, nw, st_ref, o_ref, rows, n_heads, dk, dv, masks):
    levels, pre_c, suf_c, dec = _decay_levels(w)
    kb = k.astype(BF16)
    ops = [(q.astype(BF16), kb)]
    for pre, suf in levels:
        ops.append(((q * pre).astype(BF16), kb if suf is None else (k * suf).astype(BF16)))
    qd = (q * pre_c).astype(BF16)
    kd = (k * suf_c).astype(BF16)
    for h in range(n_heads):
        sk = slice(h * dk, (h + 1) * dk)
        sv = slice(h * dv, (h + 1) * dv)
        a = None
        for (qs, ks), m in zip(ops, masks):
            part = jnp.where(m, _dot_nt(qs[:, sk], ks[:, sk]), 0.0)
            a = part if a is None else a + part
        st = st_ref[h]
        vh = v[:, sv]
        o = _dot_nt(qd[:, sk], st.astype(BF16)) + _dot(a.astype(BF16), vh)
        st_ref[h] = st * dec[:, sk] + _dot_tn(vh, kd[:, sk])
        y = o * lax.rsqrt(jnp.mean(o * o, axis=-1, keepdims=True) + EPS) * nw
        rh = r[:, sv]
        o_ref[rows, sv] = (y * (rh * jax.nn.sigmoid(rh))).astype(o_ref.dtype)


def _gla_kernel(q_ref, k_ref, v_ref, r_ref, g_ref, nw_ref, o_ref, st_ref, *, scale, n_chunks):
    @pl.when(pl.program_id(1) == 0)
    def _():
        st_ref[...] = jnp.zeros_like(st_ref)

    masks = _score_masks(CHUNK)
    n_heads, dv, dk = st_ref.shape
    nw = nw_ref[...]

    def body(ci, carry):
        rows = pl.ds(pl.multiple_of(ci * CHUNK, CHUNK), CHUNK)
        q = q_ref[rows, :].astype(F32) * scale
        k = k_ref[rows, :].astype(F32)
        _recurrence_chunk(q, k, jnp.exp(g_ref[rows, :]), v_ref[rows, :], r_ref[rows, :].astype(F32), nw,
                          st_ref, o_ref, rows, n_heads, dk, dv, masks)
        return carry

    lax.fori_loop(0, n_chunks, body, 0, unroll=2)


def _gla(proj, g, nw, batch, seq, ts):
    t, _ = proj.shape
    kd = g.shape[1]
    gw = 2 * kd
    hk = kd // GLA_HEADS
    hv = gw // GLA_HEADS
    ns = seq // ts
    row = lambda b, s: b * ns + s
    return pl.pallas_call(
        functools.partial(_gla_kernel, scale=hk ** -0.5, n_chunks=ts // CHUNK),
        grid=(batch, ns),
        in_specs=[pl.BlockSpec((ts, kd), lambda b, s: (row(b, s), 0)),
                  pl.BlockSpec((ts, kd), lambda b, s: (row(b, s), 1)),
                  pl.BlockSpec((ts, gw), lambda b, s: (row(b, s), 1)),
                  pl.BlockSpec((ts, gw), lambda b, s: (row(b, s), 2)),
                  pl.BlockSpec((ts, kd), lambda b, s: (row(b, s), 0)),
                  pl.BlockSpec((1, hv), lambda b, s: (0, 0))],
        out_specs=pl.BlockSpec((ts, gw), lambda b, s: (row(b, s), 0)),
        out_shape=jax.ShapeDtypeStruct((t, gw), BF16),
        scratch_shapes=[pltpu.VMEM((GLA_HEADS, hv, hk), F32)],
        compiler_params=_params(("parallel", "arbitrary")),
        name="gla",
    )(proj, proj, proj, proj, g, nw)


def _hgrn_kernel(q_ref, f_ref, i_ref, r_ref, lb_ref, nw_ref, o_ref, st_ref, *, n_chunks):
    @pl.when(pl.program_id(1) == 0)
    def _():
        st_ref[...] = jnp.zeros_like(st_ref)

    masks = _score_masks(CHUNK)
    n_heads = st_ref.shape[0]
    lb = lb_ref[...]
    nw = nw_ref[...]

    def body(ci, carry):
        rows = pl.ds(pl.multiple_of(ci * CHUNK, CHUNK), CHUNK)
        zf = f_ref[rows, :].astype(F32)
        forget = lb + (1.0 - lb) * jax.nn.sigmoid(zf)
        k = (1.0 - lb) * jax.nn.sigmoid(-zf)
        qz = q_ref[rows, :].astype(F32)
        q = qz * jax.nn.sigmoid(qz)
        _recurrence_chunk(q, k, forget, i_ref[rows, :], r_ref[rows, :].astype(F32), nw,
                          st_ref, o_ref, rows, n_heads, HGRN_EXPAND, HGRN_EXPAND, masks)
        return carry

    lax.fori_loop(0, n_chunks, body, 0, unroll=2)


def _hgrn(proj, lb, nw, batch, seq, ts, hw, col0):
    t, _ = proj.shape
    ns = seq // ts
    n_heads = hw // HGRN_EXPAND
    row = lambda b, s: b * ns + s
    return pl.pallas_call(
        functools.partial(_hgrn_kernel, n_chunks=ts // CHUNK),
        grid=(batch, ns),
        in_specs=[pl.BlockSpec((ts, hw), lambda b, s: (row(b, s), col0)),
                  pl.BlockSpec((ts, hw), lambda b, s: (row(b, s), col0 + 1)),
                  pl.BlockSpec((ts, hw), lambda b, s: (row(b, s), col0 + 2)),
                  pl.BlockSpec((ts, hw), lambda b, s: (row(b, s), col0 + 3)),
                  pl.BlockSpec((1, hw), lambda b, s: (0, 0)),
                  pl.BlockSpec((1, HGRN_EXPAND), lambda b, s: (0, 0))],
        out_specs=pl.BlockSpec((ts, hw), lambda b, s: (row(b, s), 0)),
        out_shape=jax.ShapeDtypeStruct((t, hw), BF16),
        scratch_shapes=[pltpu.VMEM((n_heads, HGRN_EXPAND, HGRN_EXPAND), F32)],
        compiler_params=_params(("parallel", "arbitrary")),
        name="hgrn",
    )(proj, proj, proj, proj, lb, nw)


def _out_proj_kernel(oa_ref, ob_ref, wa_ref, wb_ref, x_ref, h_ref):
    h_ref[...] = x_ref[...] + _dot(oa_ref[...], wa_ref[...]) + _dot(ob_ref[...], wb_ref[...])


def _out_proj(oa, ob, w, x2, tm, tn):
    t, d = x2.shape
    ka, kb = oa.shape[1], ob.shape[1]
    assert ka == kb and w.shape[0] == ka + kb
    return pl.pallas_call(
        _out_proj_kernel,
        grid=(d // tn, t // tm),
        in_specs=[pl.BlockSpec((tm, ka), lambda j, i: (i, 0)),
                  pl.BlockSpec((tm, kb), lambda j, i: (i, 0)),
                  pl.BlockSpec((ka, tn), lambda j, i: (0, j)),
                  pl.BlockSpec((kb, tn), lambda j, i: (1, j)),
                  pl.BlockSpec((tm, tn), lambda j, i: (i, j))],
        out_specs=pl.BlockSpec((tm, tn), lambda j, i: (i, j)),
        out_shape=jax.ShapeDtypeStruct((t, d), F32),
        compiler_params=_params(("parallel", "parallel")),
        name="out_proj",
    )(oa, ob, w, w, x2)


def _router_kernel(h_ref, nw_ref, rw_ref, rb_ref, topi_ref, topw_ref, rank_ref, cnt_ref, run_ref):
    step = pl.program_id(0)

    @pl.when(step == 0)
    def _():
        run_ref[...] = jnp.zeros_like(run_ref)

    tr = h_ref.shape[0]
    xn = _rmsnorm(h_ref[...], nw_ref[...])
    logits = jnp.dot(xn, rw_ref[...], precision=lax.Precision.HIGHEST,
                     preferred_element_type=F32) + rb_ref[...]
    lane = lax.broadcasted_iota(I32, (tr, LANES), 1)
    neg = jnp.float32(-jnp.inf)
    cur = jnp.where(lane < N_EXPERTS, logits, neg)
    vals, sels = [], []
    topi = jnp.zeros((tr, LANES), I32)
    for kk in range(TOP_K):
        m = jnp.max(cur, axis=-1, keepdims=True)
        idx = jnp.min(jnp.where(cur == m, lane, LANES), axis=-1, keepdims=True)
        sel = lane == idx
        vals.append(m)
        sels.append(sel)
        topi = jnp.where(lane == kk, idx, topi)
        cur = jnp.where(sel, neg, cur)
    exps = [jnp.exp(v - vals[0]) for v in vals]
    denom = exps[0] + exps[1] + exps[2] + exps[3]
    topw = jnp.zeros((tr, LANES), F32)
    hot = jnp.zeros((tr, LANES), F32)
    for kk in range(TOP_K):
        topw = jnp.where(lane == kk, exps[kk] / denom, topw)
        hot = hot + sels[kk].astype(F32)
    r_i = lax.broadcasted_iota(I32, (tr, tr), 0)
    c_i = lax.broadcasted_iota(I32, (tr, tr), 1)
    lower = (c_i < r_i).astype(BF16)
    before = _dot(lower, hot.astype(BF16)) + run_ref[...]
    rank = jnp.zeros((tr, LANES), F32)
    for kk in range(TOP_K):
        rk = jnp.sum(jnp.where(sels[kk], before, 0.0), axis=-1, keepdims=True)
        rank = jnp.where(lane == kk, rk, rank)
    topi_ref[...] = topi
    topw_ref[...] = topw
    rank_ref[...] = rank.astype(I32)
    run_ref[...] = run_ref[...] + jnp.sum(hot, axis=0, keepdims=True)
    cnt_ref[...] = run_ref[...]


def _router(h1, nw, rw, rb, tr):
    t, d = h1.shape
    tile = lambda i: (i, 0)
    fixed = lambda i: (0, 0)
    return pl.pallas_call(
        _router_kernel,
        grid=(t // tr,),
        in_specs=[pl.BlockSpec((tr, d), tile),
                  pl.BlockSpec((1, d), fixed),
                  pl.BlockSpec((d, LANES), fixed),
                  pl.BlockSpec((1, LANES), fixed)],
        out_specs=[pl.BlockSpec((tr, LANES), tile),
                   pl.BlockSpec((tr, LANES), tile),
                   pl.BlockSpec((tr, LANES), tile),
                   pl.BlockSpec((1, LANES), fixed)],
        out_shape=[jax.ShapeDtypeStruct((t, LANES), I32),
                   jax.ShapeDtypeStruct((t, LANES), F32),
                   jax.ShapeDtypeStruct((t, LANES), I32),
                   jax.ShapeDtypeStruct((1, LANES), F32)],
        scratch_shapes=[pltpu.VMEM((1, LANES), F32)],
        compiler_params=_params(("arbitrary",)),
        name="router",
    )(h1, nw, rw, rb)


def _zero_rows(pad_ref, xs_ref, zero_ref, sem, n_spans, wait):
    zr = zero_ref.shape[0]

    def run(copy):
        copy.wait() if wait else copy.start()

    def span(e, carry):
        start = pad_ref[0, 0, e]

        def one(j, c2):
            run(pltpu.make_async_copy(zero_ref.at[pl.ds(0, 1)], xs_ref.at[pl.ds(start + j, 1)], sem))
            return c2

        lax.fori_loop(0, pad_ref[0, 0, n_spans + e], one, 0)
        return carry

    lax.fori_loop(0, n_spans - 1, span, 0)
    tail = pad_ref[0, 0, n_spans - 1]

    def big(j, carry):
        run(pltpu.make_async_copy(zero_ref, xs_ref.at[pl.ds(pl.multiple_of(tail + j * zr, zr), zr)], sem))
        return carry

    lax.fori_loop(0, pad_ref[0, 0, 2 * n_spans - 1] // zr, big, 0)


def _dispatch_kernel(pos_ref, pad_ref, h_ref, nw_ref, xs_ref, buf_ref, zero_ref, sem, zsem, *, n_spans):
    i = pl.program_id(0)
    n = pl.num_programs(0)
    td = h_ref.shape[0]
    slot = i % 2

    def wait_rows(sl):
        for _ in range(TOP_K):
            pltpu.make_async_copy(buf_ref.at[sl], xs_ref.at[pl.ds(0, td)], sem.at[sl]).wait()

    @pl.when(i == 0)
    def _():
        zeros = jnp.zeros(zero_ref.shape, F32)
        zero_ref[...] = _pack2(zeros, zeros)
        _zero_rows(pad_ref, xs_ref, zero_ref, zsem, n_spans, wait=False)

    @pl.when(i >= 2)
    def _():
        wait_rows(slot)

    buf_ref[slot] = _pack_bf16_pairs(_rmsnorm(h_ref[...], nw_ref[...]))

    def issue(ti, carry):
        for kk in range(TOP_K):
            p = pos_ref[0, 0, kk * td + ti]
            pltpu.make_async_copy(buf_ref.at[slot, pl.ds(ti, 1)], xs_ref.at[pl.ds(p, 1)], sem.at[slot]).start()
        return carry

    lax.fori_loop(0, td, issue, 0)

    @pl.when(i == 0)
    def _():
        _zero_rows(pad_ref, xs_ref, zero_ref, zsem, n_spans, wait=True)

    @pl.when(i == n - 1)
    def _():
        wait_rows(slot)

    @pl.when((i == n - 1) & (n > 1))
    def _():
        wait_rows(1 - slot)


def _dispatch(pos_tiles, pad_spans, h1, nw, n_rows, td):
    t, d = h1.shape
    n_spans = pad_spans.shape[2] // 2
    return pl.pallas_call(
        functools.partial(_dispatch_kernel, n_spans=n_spans),
        grid=(t // td,),
        in_specs=[pl.BlockSpec((1, 1, TOP_K * td), lambda i: (i, 0, 0), memory_space=pltpu.SMEM),
                  pl.BlockSpec((1, 1, 2 * n_spans), lambda i: (0, 0, 0), memory_space=pltpu.SMEM),
                  pl.BlockSpec((td, d), lambda i: (i, 0)),
                  pl.BlockSpec((1, d), lambda i: (0, 0))],
        out_specs=pl.BlockSpec(memory_space=pl.ANY),
        out_shape=jax.ShapeDtypeStruct((n_rows, d // 2), U32),
        scratch_shapes=[pltpu.VMEM((2, td, d // 2), U32), pltpu.VMEM((td, d // 2), U32),
                        pltpu.SemaphoreType.DMA((2,)), pltpu.SemaphoreType.DMA],
        compiler_params=_params(("arbitrary",)),
        name="dispatch",
    )(pos_tiles, pad_spans, h1, nw)


def _experts_kernel(te_ref, tb_ref, nu_ref, x_ref, gw_ref, uw_ref, gb_ref, ub_ref,
                    dlo_ref, dhi_ref, blo_ref, bhi_ref, y_ref, xb_ref, hid_ref, *, nf):
    del te_ref, tb_ref
    i = pl.program_id(0)
    s = pl.program_id(1)
    half = x_ref.shape[1]
    tf = gw_ref.shape[3]
    active = i < nu_ref[0]

    @pl.when(active & (s == 0))
    def _():
        u = x_ref[...]
        xb_ref[:, :half] = _unpack_lo(u).astype(BF16)
        xb_ref[:, half:] = _unpack_hi(u).astype(BF16)

    @pl.when(active & (s < nf))
    def _():
        xb = xb_ref[...]
        g = jnp.minimum(_dot(xb, gw_ref[0, 0]) + gb_ref[0], SWIGLU_LIMIT)
        u = jnp.clip(_dot(xb, uw_ref[0, 0]) + ub_ref[0], -SWIGLU_LIMIT, SWIGLU_LIMIT)
        hid_ref[s] = (g * jax.nn.sigmoid(SWIGLU_ALPHA * g) * (u + 1.0)).astype(BF16)

    @pl.when(active & (s >= nf))
    def _():
        lo = blo_ref[0]
        hi = bhi_ref[0]
        for f in range(nf):
            hf = hid_ref[f]
            lo = lo + _dot(hf, dlo_ref[0, 0, f * tf:(f + 1) * tf, :])
            hi = hi + _dot(hf, dhi_ref[0, 0, f * tf:(f + 1) * tf, :])
        y_ref[...] = _pack2(lo, hi)

    @pl.when(jnp.logical_not(active) & (s >= nf))
    def _():
        zeros = jnp.zeros(y_ref.shape, F32)
        y_ref[...] = _pack2(zeros, zeros)


def _block_columns(w, width):
    e, rows, cols = w.shape
    return w.reshape(e, rows, cols // width, width).transpose(0, 2, 1, 3).astype(BF16)


def _experts(tile_expert, tile_block, n_used, xs, gw, gb, uw, ub, dw, db, tm):
    p, half = xs.shape
    d = 2 * half
    _, nf, _, tf = gw.shape
    _, nn2, ff, tn = dw.shape
    nn = nn2 // 2
    n_tiles = p // tm

    def fidx(i, s, nu):
        return jnp.where(i < nu[0], jnp.minimum(s, nf - 1), nf - 1)

    def nidx(i, s, nu):
        return jnp.where(i < nu[0], jnp.maximum(s - nf, 0), nn - 1)

    grid_spec = pltpu.PrefetchScalarGridSpec(
        num_scalar_prefetch=3,
        grid=(n_tiles, nf + nn),
        in_specs=[pl.BlockSpec((tm, half), lambda i, s, te, tb, nu: (tb[i], 0)),
                  pl.BlockSpec((1, 1, d, tf), lambda i, s, te, tb, nu: (te[i], fidx(i, s, nu), 0, 0)),
                  pl.BlockSpec((1, 1, d, tf), lambda i, s, te, tb, nu: (te[i], fidx(i, s, nu), 0, 0)),
                  pl.BlockSpec((1, 1, tf), lambda i, s, te, tb, nu: (te[i], 0, fidx(i, s, nu))),
                  pl.BlockSpec((1, 1, tf), lambda i, s, te, tb, nu: (te[i], 0, fidx(i, s, nu))),
                  pl.BlockSpec((1, 1, ff, tn), lambda i, s, te, tb, nu: (te[i], nidx(i, s, nu), 0, 0)),
                  pl.BlockSpec((1, 1, ff, tn), lambda i, s, te, tb, nu: (te[i], nn + nidx(i, s, nu), 0, 0)),
                  pl.BlockSpec((1, 1, tn), lambda i, s, te, tb, nu: (te[i], 0, nidx(i, s, nu))),
                  pl.BlockSpec((1, 1, tn), lambda i, s, te, tb, nu: (te[i], 0, nn + nidx(i, s, nu)))],
        out_specs=pl.BlockSpec((tm, tn), lambda i, s, te, tb, nu: (i, jnp.maximum(s - nf, 0))),
        scratch_shapes=[pltpu.VMEM((tm, d), BF16), pltpu.VMEM((nf, tm, tf), BF16)],
    )
    return pl.pallas_call(
        functools.partial(_experts_kernel, nf=nf),
        grid_spec=grid_spec,
        out_shape=jax.ShapeDtypeStruct((p, half), U32),
        compiler_params=_params(("arbitrary", "arbitrary")),
        name="experts",
    )(tile_expert, tile_block, n_used, xs, gw, uw, gb, ub, dw, dw, db, db)


def _combine_kernel(pos_ref, nxt_ref, h_ref, w_ref, nw_ref, y_ref, o_ref, buf_ref, sem):
    i = pl.program_id(0)
    n = pl.num_programs(0)
    tc = h_ref.shape[0]
    half = buf_ref.shape[3]
    slot = i % 2

    def gather(idx_ref, sl):
        def issue(ti, carry):
            for kk in range(TOP_K):
                p = idx_ref[0, 0, kk * tc + ti]
                pltpu.make_async_copy(y_ref.at[pl.ds(p, 1)], buf_ref.at[sl, kk, pl.ds(ti, 1)], sem.at[sl]).start()
            return carry

        lax.fori_loop(0, tc, issue, 0)

    @pl.when(i == 0)
    def _():
        gather(pos_ref, slot)

    @pl.when(i + 1 < n)
    def _():
        gather(nxt_ref, 1 - slot)

    for kk in range(TOP_K):
        pltpu.make_async_copy(y_ref.at[pl.ds(0, tc)], buf_ref.at[slot, kk], sem.at[slot]).wait()

    w = w_ref[...]
    h = h_ref[...]
    lo = h[:, :half]
    hi = h[:, half:]
    for kk in range(TOP_K):
        u = buf_ref[slot, kk]
        wk = w[:, kk:kk + 1]
        lo = lo + wk * _unpack_lo(u)
        hi = hi + wk * _unpack_hi(u)
    ms = (jnp.sum(lo * lo, axis=-1, keepdims=True) + jnp.sum(hi * hi, axis=-1, keepdims=True)) / (2 * half)
    inv = lax.rsqrt(ms + EPS)
    nw = nw_ref[...]
    o_ref[:, :half] = lo * inv * nw[:, :half]
    o_ref[:, half:] = hi * inv * nw[:, half:]


def _combine(pos_tiles, h1, topw, nw, y, tc):
    t, d = h1.shape
    n = t // tc
    return pl.pallas_call(
        _combine_kernel,
        grid=(n,),
        in_specs=[pl.BlockSpec((1, 1, TOP_K * tc), lambda i: (i, 0, 0), memory_space=pltpu.SMEM),
                  pl.BlockSpec((1, 1, TOP_K * tc), lambda i: (jnp.minimum(i + 1, n - 1), 0, 0),
                               memory_space=pltpu.SMEM),
                  pl.BlockSpec((tc, d), lambda i: (i, 0)),
                  pl.BlockSpec((tc, LANES), lambda i: (i, 0)),
                  pl.BlockSpec((1, d), lambda i: (0, 0)),
                  pl.BlockSpec(memory_space=pl.ANY)],
        out_specs=pl.BlockSpec((tc, d), lambda i: (i, 0)),
        out_shape=jax.ShapeDtypeStruct((t, d), F32),
        scratch_shapes=[pltpu.VMEM((2, TOP_K, tc, d // 2), U32), pltpu.SemaphoreType.DMA((2,))],
        compiler_params=_params(("arbitrary",)),
        name="combine",
    )(pos_tiles, pos_tiles, h1, topw, nw, y)


def _tiles(t, seq, d):
    return dict(
        norm_tm=min(512, t),
        proj_tm=min(1024, t), proj_tn=min(1024, d // 2),
        rec_ts=min(512, seq),
        out_tm=min(512, t), out_tn=min(1024, d),
        router_tr=min(512, t),
        scatter_td=min(256, t),
        expert_tm=min(512, t), expert_tf=min(512, d // 4), expert_tn=min(1024, d // 2),
        combine_tc=min(256, t),
    )


def _pos_tiles(pos, tile):
    t = pos.shape[0]
    return pos.reshape(t // tile, tile, TOP_K).transpose(0, 2, 1).reshape(t // tile, 1, TOP_K * tile)


def kernel(x, attn_norm_w, in_proj_w, gla_gate_up_w, gla_gate_up_b, gla_out_norm_w, hgrn_lb_logits, hgrn_out_norm_w, out_proj_w, ffn_norm_w, router_w, router_b, expert_gate_w, expert_gate_b, expert_up_w, expert_up_b, expert_down_w, expert_down_b, final_norm_w):
    batch, seq, d = x.shape
    depth = attn_norm_w.shape[0]
    assert depth == 1 and d % 2048 == 0 and seq % CHUNK == 0
    t = batch * seq
    kd = d // 4
    gw = d // 2
    hw = d - gw
    ff = expert_gate_w.shape[-1]
    ts = _tiles(t, seq, d)

    lower = jnp.cumsum(jax.nn.softmax(hgrn_lb_logits.astype(F32), axis=0), axis=0)[0:1]

    w_in = in_proj_w[0]
    c_lr = 2 * kd + 2 * gw
    w_gla = w_in[:, :c_lr].astype(BF16)
    w_hgrn = w_in[:, c_lr + GLA_RANK:].astype(BF16)
    w_lr = jnp.pad(w_in[:, c_lr:c_lr + GLA_RANK], ((0, 0), (0, LANES - GLA_RANK))).astype(BF16)
    up_w = jnp.pad(gla_gate_up_w[0], ((0, LANES - GLA_RANK), (0, 0)))
    w_out = out_proj_w[0].astype(BF16)
    r_w = jnp.pad(router_w[0], ((0, 0), (0, LANES - N_EXPERTS)))
    r_b = jnp.pad(router_b, ((0, 0), (0, LANES - N_EXPERTS)))

    x2 = x.reshape(t, d)
    xn, g_gla = _norm_gate(x2, attn_norm_w, w_lr, up_w, gla_gate_up_b, ts["norm_tm"])
    proj_a = _in_proj(xn, w_gla, ts["proj_tm"], ts["proj_tn"])
    proj_b = _in_proj(xn, w_hgrn, ts["proj_tm"], ts["proj_tn"])
    o_a = _gla(proj_a, g_gla, gla_out_norm_w, batch, seq, ts["rec_ts"])
    o_b = _hgrn(proj_b, lower, hgrn_out_norm_w, batch, seq, ts["rec_ts"], hw, 0)
    h1 = _out_proj(o_a, o_b, w_out, x2, ts["out_tm"], ts["out_tn"])

    topi, topw, rank, counts = _router(h1, ffn_norm_w, r_w, r_b, ts["router_tr"])

    tm = ts["expert_tm"]
    assert tm % ts["scatter_td"] == 0
    n_tiles = (t * TOP_K) // tm + N_EXPERTS
    cnt = counts[0, :N_EXPERTS].astype(I32)
    tiles_e = (cnt + tm - 1) // tm
    tile_end = jnp.cumsum(tiles_e)
    row_start = (tile_end - tiles_e) * tm
    n_used = tile_end[-1]
    pos = row_start[topi[:, :TOP_K]] + rank[:, :TOP_K]
    tile_id = jnp.arange(n_tiles, dtype=I32)
    tile_block = jnp.minimum(tile_id, n_used - 1)
    tile_expert = jnp.sum((tile_block[:, None] >= tile_end[None, :]).astype(I32), axis=1)

    pad_start = jnp.concatenate([row_start + cnt, (n_used * tm)[None]])
    pad_len = jnp.concatenate([tiles_e * tm - cnt, ((n_tiles - n_used) * tm)[None]])
    pad_spans = jnp.concatenate([pad_start, pad_len]).astype(I32)[None, None, :]
    xs = _dispatch(_pos_tiles(pos, ts["scatter_td"]), pad_spans, h1, ffn_norm_w, n_tiles * tm, ts["scatter_td"])
    y = _experts(tile_expert, tile_block, n_used.reshape(1), xs,
                 _block_columns(expert_gate_w[0], ts["expert_tf"]), expert_gate_b[0][:, None, :],
                 _block_columns(expert_up_w[0], ts["expert_tf"]), expert_up_b[0][:, None, :],
                 _block_columns(expert_down_w[0], ts["expert_tn"]), expert_down_b[0][:, None, :], tm)
    out = _combine(_pos_tiles(pos, ts["combine_tc"]), h1, topw, final_norm_w[None, :], y, ts["combine_tc"])
    return out.reshape(batch, seq, d)
```

```python
import functools

import jax
import jax.numpy as jnp
from jax import lax
from jax.experimental import pallas as pl
from jax.experimental.pallas import tpu as pltpu

CHUNK = 64
N_EXPERTS = 32
TOP_K = 4
GLA_HEADS = 4
GLA_RANK = 16
GLA_GATE_NORM = 16.0
HGRN_EXPAND = 128
SWIGLU_ALPHA = 1.702
SWIGLU_LIMIT = 7.0
EPS = 1e-5
LANES = 128
VMEM_LIMIT = 56 * 1024 * 1024

F32 = jnp.float32
BF16 = jnp.bfloat16
U32 = jnp.uint32
I32 = jnp.int32


def _params(semantics):
    return pltpu.CompilerParams(dimension_semantics=semantics, vmem_limit_bytes=VMEM_LIMIT)


def _dot(a, b):
    return jnp.dot(a, b, preferred_element_type=F32)


def _dot_nt(a, b):
    return lax.dot_general(a, b, (((1,), (1,)), ((), ())), preferred_element_type=F32)


def _dot_tn(a, b):
    return lax.dot_general(a, b, (((0,), (0,)), ((), ())), preferred_element_type=F32)


def _rmsnorm(x, w):
    return x * lax.rsqrt(jnp.mean(x * x, axis=-1, keepdims=True) + EPS) * w


def _pack_bf16_pairs(x):
    n = x.shape[1] // 2
    return _pack2(x[:, :n], x[:, n:])


def _pack2(lo, hi):
    return pltpu.pack_elementwise([lo, hi], packed_dtype=BF16)


def _unpack_lo(u):
    return pltpu.unpack_elementwise(u, index=0, packed_dtype=BF16, unpacked_dtype=F32)


def _unpack_hi(u):
    return pltpu.unpack_elementwise(u, index=1, packed_dtype=BF16, unpacked_dtype=F32)


def _norm_gate_kernel(x_ref, nw_ref, wlr_ref, upw_ref, upb_ref, xn_ref, g_ref):
    xn = _rmsnorm(x_ref[...], nw_ref[...]).astype(BF16)
    xn_ref[...] = xn
    lr = _dot(xn, wlr_ref[...])
    z = jnp.dot(lr, upw_ref[...], precision=lax.Precision.HIGHEST,
                preferred_element_type=F32) + upb_ref[...]
    log_sig = jnp.minimum(z, 0.0) - jnp.log1p(jnp.exp(-jnp.abs(z)))
    g_ref[...] = log_sig * (1.0 / GLA_GATE_NORM)


def _norm_gate(x2, nw, wlr, upw, upb, tm):
    t, d = x2.shape
    kd = upw.shape[1]
    return pl.pallas_call(
        _norm_gate_kernel,
        grid=(t // tm,),
        in_specs=[pl.BlockSpec((tm, d), lambda i: (i, 0)),
                  pl.BlockSpec((1, d), lambda i: (0, 0)),
                  pl.BlockSpec((d, LANES), lambda i: (0, 0)),
                  pl.BlockSpec((LANES, kd), lambda i: (0, 0)),
                  pl.BlockSpec((1, kd), lambda i: (0, 0))],
        out_specs=[pl.BlockSpec((tm, d), lambda i: (i, 0)),
                   pl.BlockSpec((tm, kd), lambda i: (i, 0))],
        out_shape=[jax.ShapeDtypeStruct((t, d), BF16), jax.ShapeDtypeStruct((t, kd), F32)],
        compiler_params=_params(("parallel",)),
        name="norm_gate",
    )(x2, nw, wlr, upw, upb)


def _matmul_kernel(a_ref, b_ref, o_ref):
    o_ref[...] = _dot(a_ref[...], b_ref[...]).astype(o_ref.dtype)


def _in_proj(xn, w, tm, tn):
    t, d = xn.shape
    n = w.shape[1]
    return pl.pallas_call(
        _matmul_kernel,
        grid=(n // tn, t // tm),
        in_specs=[pl.BlockSpec((tm, d), lambda j, i: (i, 0)),
                  pl.BlockSpec((d, tn), lambda j, i: (0, j))],
        out_specs=pl.BlockSpec((tm, tn), lambda j, i: (i, j)),
        out_shape=jax.ShapeDtypeStruct((t, n), BF16),
        compiler_params=_params(("parallel", "parallel")),
        name="in_proj",
    )(xn, w)


def _decay_levels(w):
    c = w.shape[0]
    row = lax.broadcasted_iota(I32, (c, 1), 0)
    pre, suf, tot = w, None, w
    levels = []
    s = 1
    while s < 8:
        levels.append((pre, suf))
        odd = (row & s) != 0
        prev_tot = pltpu.roll(tot, s, 0)
        next_tot = pltpu.roll(tot, c - s, 0)
        pre = pre * jnp.where(odd, prev_tot, 1.0)
        grow = jnp.where(odd, 1.0, next_tot)
        suf = grow if suf is None else suf * grow
        tot = tot * jnp.where(odd, prev_tot, next_tot)
        s *= 2
    groups = c // 8
    pre_g = [pre[8 * r:8 * r + 8] for r in range(groups)]
    suf_g = [suf[8 * r:8 * r + 8] for r in range(groups)]
    tot_g = [tot[8 * r:8 * r + 8] for r in range(groups)]
    m = 1
    while 8 * m < c:
        levels.append((jnp.concatenate(pre_g, axis=0), jnp.concatenate(suf_g, axis=0)))
        new_tot = []
        for r in range(groups):
            if (r // m) & 1:
                pre_g[r] = pre_g[r] * tot_g[r - m]
                new_tot.append(tot_g[r] * tot_g[r - m])
            else:
                suf_g[r] = suf_g[r] * tot_g[r + m]
                new_tot.append(tot_g[r] * tot_g[r + m])
        tot_g = new_tot
        m *= 2
    return levels, jnp.concatenate(pre_g, axis=0), jnp.concatenate(suf_g, axis=0), tot_g[0][0:1]


def _score_masks(c):
    row = lax.broadcasted_iota(I32, (c, c), 0)
    col = lax.broadcasted_iota(I32, (c, c), 1)
    masks = [row == col]
    s, sh = 1, 0
    while s < c:
        masks.append((((row ^ col) >> sh) == 1) & ((row & s) != 0))
        s *= 2
        sh += 1
    return masks


def _recurrence_chunk(q, k, w, v, r, nw, st_ref, o_ref, rows, n_heads, dk, dv, masks):
    levels, pre_c, suf_c, dec = _decay_levels(w)
    kb = k.astype(BF16)
    ops = [(q.astype(BF16), kb)]
    for pre, suf in levels:
        ops.append(((q * pre).astype(BF16), kb if suf is None else (k * suf).astype(BF16)))
    qd = (q * pre_c).astype(BF16)
    kd = (k * suf_c).astype(BF16)
    for h in range(n_heads):
        sk = slice(h * dk, (h + 1) * dk)
        sv = slice(h * dv, (h + 1) * dv)
        a = None
        for (qs, ks), m in zip(ops, masks):
            part = jnp.where(m, _dot_nt(qs[:, sk], ks[:, sk]), 0.0)
            a = part if a is None else a + part
        st = st_ref[h]
        vh = v[:, sv]
        o = _dot_nt(qd[:, sk], st.astype(BF16)) + _dot(a.astype(BF16), vh)
        st_ref[h] = st * dec[:, sk] + _dot_tn(vh, kd[:, sk])
        y = o * lax.rsqrt(jnp.mean(o * o, axis=-1, keepdims=True) + EPS) * nw
        rh = r[:, sv]
        o_ref[rows, sv] = (y * (rh * jax.nn.sigmoid(rh))).astype(o_ref.dtype)


def _gla_kernel(q_ref, k_ref, v_ref, r_ref, g_ref, nw_ref, o_ref, st_ref, *, scale, n_chunks):
    @pl.when(pl.program_id(1) == 0)
    def _():
        st_ref[...] = jnp.zeros_like(st_ref)

    masks = _score_masks(CHUNK)
    n_heads, dv, dk = st_ref.shape
    nw = nw_ref[...]

    def body(ci, carry):
        rows = pl.ds(pl.multiple_of(ci * CHUNK, CHUNK), CHUNK)
        q = q_ref[rows, :].astype(F32) * scale
        k = k_ref[rows, :].astype(F32)
        _recurrence_chunk(q, k, jnp.exp(g_ref[rows, :]), v_ref[rows, :], r_ref[rows, :].astype(F32), nw,
                          st_ref, o_ref, rows, n_heads, dk, dv, masks)
        return carry

    lax.fori_loop(0, n_chunks, body, 0, unroll=2)


def _gla(proj, g, nw, batch, seq, ts):
    t, _ = proj.shape
    kd = g.shape[1]
    gw = 2 * kd
    hk = kd // GLA_HEADS
    hv = gw // GLA_HEADS
    ns = seq // ts
    row = lambda b, s: b * ns + s
    return pl.pallas_call(
        functools.partial(_gla_kernel, scale=hk ** -0.5, n_chunks=ts // CHUNK),
        grid=(batch, ns),
        in_specs=[pl.BlockSpec((ts, kd), lambda b, s: (row(b, s), 0)),
                  pl.BlockSpec((ts, kd), lambda b, s: (row(b, s), 1)),
                  pl.BlockSpec((ts, gw), lambda b, s: (row(b, s), 1)),
                  pl.BlockSpec((ts, gw), lambda b, s: (row(b, s), 2)),
                  pl.BlockSpec((ts, kd), lambda b, s: (row(b, s), 0)),
                  pl.BlockSpec((1, hv), lambda b, s: (0, 0))],
        out_specs=pl.BlockSpec((ts, gw), lambda b, s: (row(b, s), 0)),
        out_shape=jax.ShapeDtypeStruct((t, gw), BF16),
        scratch_shapes=[pltpu.VMEM((GLA_HEADS, hv, hk), F32)],
        compiler_params=_params(("parallel", "arbitrary")),
        name="gla",
    )(proj, proj, proj, proj, g, nw)


def _hgrn_kernel(q_ref, f_ref, i_ref, r_ref, lb_ref, nw_ref, o_ref, st_ref, *, n_chunks):
    @pl.when(pl.program_id(1) == 0)
    def _():
        st_ref[...] = jnp.zeros_like(st_ref)

    masks = _score_masks(CHUNK)
    n_heads = st_ref.shape[0]
    lb = lb_ref[...]
    nw = nw_ref[...]

    def body(ci, carry):
        rows = pl.ds(pl.multiple_of(ci * CHUNK, CHUNK), CHUNK)
        zf = f_ref[rows, :].astype(F32)
        forget = lb + (1.0 - lb) * jax.nn.sigmoid(zf)
        k = (1.0 - lb) * jax.nn.sigmoid(-zf)
        qz = q_ref[rows, :].astype(F32)
        q = qz * jax.nn.sigmoid(qz)
        _recurrence_chunk(q, k, forget, i_ref[rows, :], r_ref[rows, :].astype(F32), nw,
                          st_ref, o_ref, rows, n_heads, HGRN_EXPAND, HGRN_EXPAND, masks)
        return carry

    lax.fori_loop(0, n_chunks, body, 0, unroll=2)


def _hgrn(proj, lb, nw, batch, seq, ts, hw, col0):
    t, _ = proj.shape
    ns = seq // ts
    n_heads = hw // HGRN_EXPAND
    row = lambda b, s: b * ns + s
    return pl.pallas_call(
        functools.partial(_hgrn_kernel, n_chunks=ts // CHUNK),
        grid=(batch, ns),
        in_specs=[pl.BlockSpec((ts, hw), lambda b, s: (row(b, s), col0)),
                  pl.BlockSpec((ts, hw), lambda b, s: (row(b, s), col0 + 1)),
                  pl.BlockSpec((ts, hw), lambda b, s: (row(b, s), col0 + 2)),
                  pl.BlockSpec((ts, hw), lambda b, s: (row(b, s), col0 + 3)),
                  pl.BlockSpec((1, hw), lambda b, s: (0, 0)),
                  pl.BlockSpec((1, HGRN_EXPAND), lambda b, s: (0, 0))],
        out_specs=pl.BlockSpec((ts, hw), lambda b, s: (row(b, s), 0)),
        out_shape=jax.ShapeDtypeStruct((t, hw), BF16),
        scratch_shapes=[pltpu.VMEM((n_heads, HGRN_EXPAND, HGRN_EXPAND), F32)],
        compiler_params=_params(("parallel", "arbitrary")),
        name="hgrn",
    )(proj, proj, proj, proj, lb, nw)


def _out_proj_kernel(oa_ref, ob_ref, wa_ref, wb_ref, x_ref, h_ref):
    h_ref[...] = x_ref[...] + _dot(oa_ref[...], wa_ref[...]) + _dot(ob_ref[...], wb_ref[...])


def _out_proj(oa, ob, w, x2, tm, tn):
    t, d = x2.shape
    ka, kb = oa.shape[1], ob.shape[1]
    assert ka == kb and w.shape[0] == ka + kb
    return pl.pallas_call(
        _out_proj_kernel,
        grid=(d // tn, t // tm),
        in_specs=[pl.BlockSpec((tm, ka), lambda j, i: (i, 0)),
                  pl.BlockSpec((tm, kb), lambda j, i: (i, 0)),
                  pl.BlockSpec((ka, tn), lambda j, i: (0, j)),
                  pl.BlockSpec((kb, tn), lambda j, i: (1, j)),
                  pl.BlockSpec((tm, tn), lambda j, i: (i, j))],
        out_specs=pl.BlockSpec((tm, tn), lambda j, i: (i, j)),
        out_shape=jax.ShapeDtypeStruct((t, d), F32),
        compiler_params=_params(("parallel", "parallel")),
        name="out_proj",
    )(oa, ob, w, w, x2)


def _router_kernel(h_ref, nw_ref, rw_ref, rb_ref, topi_ref, topw_ref, rank_ref, cnt_ref, run_ref):
    step = pl.program_id(0)

    @pl.when(step == 0)
    def _():
        run_ref[...] = jnp.zeros_like(run_ref)

    tr = h_ref.shape[0]
    xn = _rmsnorm(h_ref[...], nw_ref[...])
    logits = jnp.dot(xn, rw_ref[...], precision=lax.Precision.HIGHEST,
                     preferred_element_type=F32) + rb_ref[...]
    lane = lax.broadcasted_iota(I32, (tr, LANES), 1)
    neg = jnp.float32(-jnp.inf)
    cur = jnp.where(lane < N_EXPERTS, logits, neg)
    vals, sels = [], []
    topi = jnp.zeros((tr, LANES), I32)
    for kk in range(TOP_K):
        m = jnp.max(cur, axis=-1, keepdims=True)
        idx = jnp.min(jnp.where(cur == m, lane, LANES), axis=-1, keepdims=True)
        sel = lane == idx
        vals.append(m)
        sels.append(sel)
        topi = jnp.where(lane == kk, idx, topi)
        cur = jnp.where(sel, neg, cur)
    exps = [jnp.exp(v - vals[0]) for v in vals]
    denom = exps[0] + exps[1] + exps[2] + exps[3]
    topw = jnp.zeros((tr, LANES), F32)
    hot = jnp.zeros((tr, LANES), F32)
    for kk in range(TOP_K):
        topw = jnp.where(lane == kk, exps[kk] / denom, topw)
        hot = hot + sels[kk].astype(F32)
    r_i = lax.broadcasted_iota(I32, (tr, tr), 0)
    c_i = lax.broadcasted_iota(I32, (tr, tr), 1)
    lower = (c_i < r_i).astype(BF16)
    before = _dot(lower, hot.astype(BF16)) + run_ref[...]
    rank = jnp.zeros((tr, LANES), F32)
    for kk in range(TOP_K):
        rk = jnp.sum(jnp.where(sels[kk], before, 0.0), axis=-1, keepdims=True)
        rank = jnp.where(lane == kk, rk, rank)
    topi_ref[...] = topi
    topw_ref[...] = topw
    rank_ref[...] = rank.astype(I32)
    run_ref[...] = run_ref[...] + jnp.sum(hot, axis=0, keepdims=True)
    cnt_ref[...] = run_ref[...]


def _router(h1, nw, rw, rb, tr):
    t, d = h1.shape
    tile = lambda i: (i, 0)
    fixed = lambda i: (0, 0)
    return pl.pallas_call(
        _router_kernel,
        grid=(t // tr,),
        in_specs=[pl.BlockSpec((tr, d), tile),
                  pl.BlockSpec((1, d), fixed),
                  pl.BlockSpec((d, LANES), fixed),
                  pl.BlockSpec((1, LANES), fixed)],
        out_specs=[pl.BlockSpec((tr, LANES), tile),
                   pl.BlockSpec((tr, LANES), tile),
                   pl.BlockSpec((tr, LANES), tile),
                   pl.BlockSpec((1, LANES), fixed)],
        out_shape=[jax.ShapeDtypeStruct((t, LANES), I32),
                   jax.ShapeDtypeStruct((t, LANES), F32),
                   jax.ShapeDtypeStruct((t, LANES), I32),
                   jax.ShapeDtypeStruct((1, LANES), F32)],
        scratch_shapes=[pltpu.VMEM((1, LANES), F32)],
        compiler_params=_params(("arbitrary",)),
        name="router",
    )(h1, nw, rw, rb)


def _zero_rows(pad_ref, xs_ref, zero_ref, sem, n_spans, wait):
    zr = zero_ref.shape[0]

    def run(copy):
        copy.wait() if wait else copy.start()

    def span(e, carry):
        start = pad_ref[0, 0, e]

        def one(j, c2):
            run(pltpu.make_async_copy(zero_ref.at[pl.ds(0, 1)], xs_ref.at[pl.ds(start + j, 1)], sem))
            return c2

        lax.fori_loop(0, pad_ref[0, 0, n_spans + e], one, 0)
        return carry

    lax.fori_loop(0, n_spans - 1, span, 0)
    tail = pad_ref[0, 0, n_spans - 1]

    def big(j, carry):
        run(pltpu.make_async_copy(zero_ref, xs_ref.at[pl.ds(pl.multiple_of(tail + j * zr, zr), zr)], sem))
        return carry

    lax.fori_loop(0, pad_ref[0, 0, 2 * n_spans - 1] // zr, big, 0)


def _dispatch_kernel(pos_ref, pad_ref, h_ref, nw_ref, xs_ref, buf_ref, zero_ref, sem, zsem, *, n_spans):
    i = pl.program_id(0)
    n = pl.num_programs(0)
    td = h_ref.shape[0]
    slot = i % 2

    def wait_rows(sl):
        for _ in range(TOP_K):
            pltpu.make_async_copy(buf_ref.at[sl], xs_ref.at[pl.ds(0, td)], sem.at[sl]).wait()

    @pl.when(i == 0)
    def _():
        zeros = jnp.zeros(zero_ref.shape, F32)
        zero_ref[...] = _pack2(zeros, zeros)
        _zero_rows(pad_ref, xs_ref, zero_ref, zsem, n_spans, wait=False)

    @pl.when(i >= 2)
    def _():
        wait_rows(slot)

    buf_ref[slot] = _pack_bf16_pairs(_rmsnorm(h_ref[...], nw_ref[...]))

    def issue(ti, carry):
        for kk in range(TOP_K):
            p = pos_ref[0, 0, kk * td + ti]
            pltpu.make_async_copy(buf_ref.at[slot, pl.ds(ti, 1)], xs_ref.at[pl.ds(p, 1)], sem.at[slot]).start()
        return carry

    lax.fori_loop(0, td, issue, 0)

    @pl.when(i == 0)
    def _():
        _zero_rows(pad_ref, xs_ref, zero_ref, zsem, n_spans, wait=True)

    @pl.when(i == n - 1)
    def _():
        wait_rows(slot)

    @pl.when((i == n - 1) & (n > 1))
    def _():
        wait_rows(1 - slot)


def _dispatch(pos_tiles, pad_spans, h1, nw, n_rows, td):
    t, d = h1.shape
    n_spans = pad_spans.shape[2] // 2
    return pl.pallas_call(
        functools.partial(_dispatch_kernel, n_spans=n_spans),
        grid=(t // td,),
        in_specs=[pl.BlockSpec((1, 1, TOP_K * td), lambda i: (i, 0, 0), memory_space=pltpu.SMEM),
                  pl.BlockSpec((1, 1, 2 * n_spans), lambda i: (0, 0, 0), memory_space=pltpu.SMEM),
                  pl.BlockSpec((td, d), lambda i: (i, 0)),
                  pl.BlockSpec((1, d), lambda i: (0, 0))],
        out_specs=pl.BlockSpec(memory_space=pl.ANY),
        out_shape=jax.ShapeDtypeStruct((n_rows, d // 2), U32),
        scratch_shapes=[pltpu.VMEM((2, td, d // 2), U32), pltpu.VMEM((td, d // 2), U32),
                        pltpu.SemaphoreType.DMA((2,)), pltpu.SemaphoreType.DMA],
        compiler_params=_params(("arbitrary",)),
        name="dispatch",
    )(pos_tiles, pad_spans, h1, nw)


def _experts_kernel(te_ref, tb_ref, nu_ref, x_ref, gw_ref, uw_ref, gb_ref, ub_ref,
                    dlo_ref, dhi_ref, blo_ref, bhi_ref, y_ref, xb_ref, hid_ref, *, nf):
    del te_ref, tb_ref
    i = pl.program_id(0)
    s = pl.program_id(1)
    half = x_ref.shape[1]
    tf = gw_ref.shape[2]
    active = i < nu_ref[0]

    @pl.when(active & (s == 0))
    def _():
        u = x_ref[...]
        xb_ref[:, :half] = _unpack_lo(u).astype(BF16)
        xb_ref[:, half:] = _unpack_hi(u).astype(BF16)

    @pl.when(active & (s < nf))
    def _():
        xb = xb_ref[...]
        g = jnp.minimum(_dot(xb, gw_ref[0]) + gb_ref[0], SWIGLU_LIMIT)
        u = jnp.clip(_dot(xb, uw_ref[0]) + ub_ref[0], -SWIGLU_LIMIT, SWIGLU_LIMIT)
        hid_ref[s] = (g * jax.nn.sigmoid(SWIGLU_ALPHA * g) * (u + 1.0)).astype(BF16)

    @pl.when(active & (s >= nf))
    def _():
        lo = blo_ref[0]
        hi = bhi_ref[0]
        for f in range(nf):
            hf = hid_ref[f]
            lo = lo + _dot(hf, dlo_ref[0, f * tf:(f + 1) * tf, :])
            hi = hi + _dot(hf, dhi_ref[0, f * tf:(f + 1) * tf, :])
        y_ref[...] = _pack2(lo, hi)

    @pl.when(jnp.logical_not(active) & (s >= nf))
    def _():
        zeros = jnp.zeros(y_ref.shape, F32)
        y_ref[...] = _pack2(zeros, zeros)


def _experts(tile_expert, tile_block, n_used, xs, gw, gb, uw, ub, dw, db, tm, tf, tn):
    p, half = xs.shape
    d = 2 * half
    ff = gw.shape[2]
    nf = ff // tf
    nn = half // tn
    n_tiles = p // tm

    def fidx(i, s, nu):
        return jnp.where(i < nu[0], jnp.minimum(s, nf - 1), nf - 1)

    def nidx(i, s, nu):
        return jnp.where(i < nu[0], jnp.maximum(s - nf, 0), nn - 1)

    def ahead(i, s):
        return jnp.where(s >= nf, jnp.minimum(i + 1, n_tiles - 1), i)

    grid_spec = pltpu.PrefetchScalarGridSpec(
        num_scalar_prefetch=3,
        grid=(n_tiles, nf + nn),
        in_specs=[pl.BlockSpec((tm, half), lambda i, s, te, tb, nu: (tb[ahead(i, s)], 0)),
                  pl.BlockSpec((1, d, tf), lambda i, s, te, tb, nu: (te[i], 0, fidx(i, s, nu))),
                  pl.BlockSpec((1, d, tf), lambda i, s, te, tb, nu: (te[ahead(i, s)], 0, fidx(ahead(i, s), jnp.where(s >= nf, 0, s), nu))),
                  pl.BlockSpec((1, 1, tf), lambda i, s, te, tb, nu: (te[i], 0, fidx(i, s, nu))),
                  pl.BlockSpec((1, 1, tf), lambda i, s, te, tb, nu: (te[i], 0, fidx(i, s, nu))),
                  pl.BlockSpec((1, ff, tn), lambda i, s, te, tb, nu: (te[i], 0, nidx(i, s, nu))),
                  pl.BlockSpec((1, ff, tn), lambda i, s, te, tb, nu: (te[i], 0, nn + nidx(i, s, nu))),
                  pl.BlockSpec((1, 1, tn), lambda i, s, te, tb, nu: (te[i], 0, nidx(i, s, nu))),
                  pl.BlockSpec((1, 1, tn), lambda i, s, te, tb, nu: (te[i], 0, nn + nidx(i, s, nu)))],
        out_specs=pl.BlockSpec((tm, tn), lambda i, s, te, tb, nu: (i, jnp.maximum(s - nf, 0))),
        scratch_shapes=[pltpu.VMEM((tm, d), BF16), pltpu.VMEM((nf, tm, tf), BF16)],
    )
    return pl.pallas_call(
        functools.partial(_experts_kernel, nf=nf),
        grid_spec=grid_spec,
        out_shape=jax.ShapeDtypeStruct((p, half), U32),
        compiler_params=_params(("arbitrary", "arbitrary")),
        name="experts",
    )(tile_expert, tile_block, n_used, xs, gw, uw, gb, ub, dw, dw, db, db)


def _combine_kernel(pos_ref, nxt_ref, h_ref, w_ref, nw_ref, y_ref, o_ref, buf_ref, sem):
    i = pl.program_id(0)
    n = pl.num_programs(0)
    tc = h_ref.shape[0]
    half = buf_ref.shape[3]
    slot = i % 2

    def gather(idx_ref, sl):
        def issue(ti, carry):
            for kk in range(TOP_K):
                p = idx_ref[0, 0, kk * tc + ti]
                pltpu.make_async_copy(y_ref.at[pl.ds(p, 1)], buf_ref.at[sl, kk, pl.ds(ti, 1)], sem.at[sl]).start()
            return carry

        lax.fori_loop(0, tc, issue, 0)

    @pl.when(i == 0)
    def _():
        gather(pos_ref, slot)

    @pl.when(i + 1 < n)
    def _():
        gather(nxt_ref, 1 - slot)

    for kk in range(TOP_K):
        pltpu.make_async_copy(y_ref.at[pl.ds(0, tc)], buf_ref.at[slot, kk], sem.at[slot]).wait()

    w = w_ref[...]
    h = h_ref[...]
    lo = h[:, :half]
    hi = h[:, half:]
    for kk in range(TOP_K):
        u = buf_ref[slot, kk]
        wk = w[:, kk:kk + 1]
        lo = lo + wk * _unpack_lo(u)
        hi = hi + wk * _unpack_hi(u)
    ms = (jnp.sum(lo * lo, axis=-1, keepdims=True) + jnp.sum(hi * hi, axis=-1, keepdims=True)) / (2 * half)
    inv = lax.rsqrt(ms + EPS)
    nw = nw_ref[...]
    o_ref[:, :half] = lo * inv * nw[:, :half]
    o_ref[:, half:] = hi * inv * nw[:, half:]


def _combine(pos_tiles, h1, topw, nw, y, tc):
    t, d = h1.shape
    n = t // tc
    return pl.pallas_call(
        _combine_kernel,
        grid=(n,),
        in_specs=[pl.BlockSpec((1, 1, TOP_K * tc), lambda i: (i, 0, 0), memory_space=pltpu.SMEM),
                  pl.BlockSpec((1, 1, TOP_K * tc), lambda i: (jnp.minimum(i + 1, n - 1), 0, 0),
                               memory_space=pltpu.SMEM),
                  pl.BlockSpec((tc, d), lambda i: (i, 0)),
                  pl.BlockSpec((tc, LANES), lambda i: (i, 0)),
                  pl.BlockSpec((1, d), lambda i: (0, 0)),
                  pl.BlockSpec(memory_space=pl.ANY)],
        out_specs=pl.BlockSpec((tc, d), lambda i: (i, 0)),
        out_shape=jax.ShapeDtypeStruct((t, d), F32),
        scratch_shapes=[pltpu.VMEM((2, TOP_K, tc, d // 2), U32), pltpu.SemaphoreType.DMA((2,))],
        compiler_params=_params(("arbitrary",)),
        name="combine",
    )(pos_tiles, pos_tiles, h1, topw, nw, y)


def _tiles(t, seq, d):
    return dict(
        norm_tm=min(512, t),
        proj_tm=min(1024, t), proj_tn=min(1024, d // 2),
        rec_ts=min(512, seq),
        out_tm=min(512, t), out_tn=min(1024, d),
        router_tr=min(512, t),
        scatter_td=min(256, t),
        expert_tm=min(512, t), expert_tf=min(512, d // 4), expert_tn=min(1024, d // 2),
        combine_tc=min(256, t),
    )


def _pos_tiles(pos, tile):
    t = pos.shape[0]
    return pos.reshape(t // tile, tile, TOP_K).transpose(0, 2, 1).reshape(t // tile, 1, TOP_K * tile)


def kernel(x, attn_norm_w, in_proj_w, gla_gate_up_w, gla_gate_up_b, gla_out_norm_w, hgrn_lb_logits, hgrn_out_norm_w, out_proj_w, ffn_norm_w, router_w, router_b, expert_gate_w, expert_gate_b, expert_up_w, expert_up_b, expert_down_w, expert_down_b, final_norm_w):
    batch, seq, d = x.shape
    depth = attn_norm_w.shape[0]
    assert depth == 1 and d % 2048 == 0 and seq % CHUNK == 0
    t = batch * seq
    kd = d // 4
    gw = d // 2
    hw = d - gw
    ff = expert_gate_w.shape[-1]
    ts = _tiles(t, seq, d)

    lower = jnp.cumsum(jax.nn.softmax(hgrn_lb_logits.astype(F32), axis=0), axis=0)[0:1]

    w_in = in_proj_w[0]
    c_lr = 2 * kd + 2 * gw
    w_main = jnp.concatenate([w_in[:, :c_lr].astype(BF16), w_in[:, c_lr + GLA_RANK:].astype(BF16)], axis=1)
    w_lr = jnp.pad(w_in[:, c_lr:c_lr + GLA_RANK], ((0, 0), (0, LANES - GLA_RANK))).astype(BF16)
    up_w = jnp.pad(gla_gate_up_w[0], ((0, LANES - GLA_RANK), (0, 0)))
    w_out = out_proj_w[0].astype(BF16)
    r_w = jnp.pad(router_w[0], ((0, 0), (0, LANES - N_EXPERTS)))
    r_b = jnp.pad(router_b, ((0, 0), (0, LANES - N_EXPERTS)))

    x2 = x.reshape(t, d)
    xn, g_gla = _norm_gate(x2, attn_norm_w, w_lr, up_w, gla_gate_up_b, ts["norm_tm"])
    proj = _in_proj(xn, w_main, ts["proj_tm"], ts["proj_tn"])
    o_a = _gla(proj, g_gla, gla_out_norm_w, batch, seq, ts["rec_ts"])
    o_b = _hgrn(proj, lower, hgrn_out_norm_w, batch, seq, ts["rec_ts"], hw, c_lr // hw)
    h1 = _out_proj(o_a, o_b, w_out, x2, ts["out_tm"], ts["out_tn"])

    topi, topw, rank, counts = _router(h1, ffn_norm_w, r_w, r_b, ts["router_tr"])

    tm = ts["expert_tm"]
    assert tm % ts["scatter_td"] == 0
    n_tiles = (t * TOP_K) // tm + N_EXPERTS
    cnt = counts[0, :N_EXPERTS].astype(I32)
    tiles_e = (cnt + tm - 1) // tm
    tile_end = jnp.cumsum(tiles_e)
    row_start = (tile_end - tiles_e) * tm
    n_used = tile_end[-1]
    pos = row_start[topi[:, :TOP_K]] + rank[:, :TOP_K]
    tile_id = jnp.arange(n_tiles, dtype=I32)
    tile_block = jnp.minimum(tile_id, n_used - 1)
    tile_expert = jnp.sum((tile_block[:, None] >= tile_end[None, :]).astype(I32), axis=1)

    pad_start = jnp.concatenate([row_start + cnt, (n_used * tm)[None]])
    pad_len = jnp.concatenate([tiles_e * tm - cnt, ((n_tiles - n_used) * tm)[None]])
    pad_spans = jnp.concatenate([pad_start, pad_len]).astype(I32)[None, None, :]
    xs = _dispatch(_pos_tiles(pos, ts["scatter_td"]), pad_spans, h1, ffn_norm_w, n_tiles * tm, ts["scatter_td"])
    y = _experts(tile_expert, tile_block, n_used.reshape(1), xs,
                 expert_gate_w[0].astype(BF16), expert_gate_b[0][:, None, :],
                 expert_up_w[0].astype(BF16), expert_up_b[0][:, None, :],
                 expert_down_w[0].astype(BF16), expert_down_b[0][:, None, :], tm, ts["expert_tf"], ts["expert_tn"])
    out = _combine(_pos_tiles(pos, ts["combine_tc"]), h1, topw, final_norm_w[None, :], y, ts["combine_tc"])
    return out.reshape(batch, seq, d)
```

```python
import functools

import jax
import jax.numpy as jnp
from jax import lax
from jax.experimental import pallas as pl
from jax.experimental.pallas import tpu as pltpu

CHUNK = 64
N_EXPERTS = 32
TOP_K = 4
GLA_HEADS = 4
GLA_RANK = 16
GLA_GATE_NORM = 16.0
HGRN_EXPAND = 128
SWIGLU_ALPHA = 1.702
SWIGLU_LIMIT = 7.0
EPS = 1e-5
LANES = 128
VMEM_LIMIT = 56 * 1024 * 1024

F32 = jnp.float32
BF16 = jnp.bfloat16
U32 = jnp.uint32
I32 = jnp.int32


def _params(semantics):
    return pltpu.CompilerParams(dimension_semantics=semantics, vmem_limit_bytes=VMEM_LIMIT)


def _dot(a, b):
    return jnp.dot(a, b, preferred_element_type=F32)


def _dot_nt(a, b):
    return lax.dot_general(a, b, (((1,), (1,)), ((), ())), preferred_element_type=F32)


def _dot_tn(a, b):
    return lax.dot_general(a, b, (((0,), (0,)), ((), ())), preferred_element_type=F32)


def _rmsnorm(x, w):
    return x * lax.rsqrt(jnp.mean(x * x, axis=-1, keepdims=True) + EPS) * w


def _pack_bf16_pairs(x):
    n = x.shape[1] // 2
    return _pack2(x[:, :n], x[:, n:])


def _pack2(lo, hi):
    return pltpu.pack_elementwise([lo, hi], packed_dtype=BF16)


def _unpack_lo(u):
    return pltpu.unpack_elementwise(u, index=0, packed_dtype=BF16, unpacked_dtype=F32)


def _unpack_hi(u):
    return pltpu.unpack_elementwise(u, index=1, packed_dtype=BF16, unpacked_dtype=F32)


def _norm_gate_kernel(x_ref, nw_ref, wlr_ref, upw_ref, upb_ref, xn_ref, g_ref):
    xn = _rmsnorm(x_ref[...], nw_ref[...]).astype(BF16)
    xn_ref[...] = xn
    lr = _dot(xn, wlr_ref[...])
    z = jnp.dot(lr, upw_ref[...], precision=lax.Precision.HIGHEST,
                preferred_element_type=F32) + upb_ref[...]
    log_sig = jnp.minimum(z, 0.0) - jnp.log1p(jnp.exp(-jnp.abs(z)))
    g_ref[...] = log_sig * (1.0 / GLA_GATE_NORM)


def _norm_gate(x2, nw, wlr, upw, upb, tm):
    t, d = x2.shape
    kd = upw.shape[1]
    return pl.pallas_call(
        _norm_gate_kernel,
        grid=(t // tm,),
        in_specs=[pl.BlockSpec((tm, d), lambda i: (i, 0)),
                  pl.BlockSpec((1, d), lambda i: (0, 0)),
                  pl.BlockSpec((d, LANES), lambda i: (0, 0)),
                  pl.BlockSpec((LANES, kd), lambda i: (0, 0)),
                  pl.BlockSpec((1, kd), lambda i: (0, 0))],
        out_specs=[pl.BlockSpec((tm, d), lambda i: (i, 0)),
                   pl.BlockSpec((tm, kd), lambda i: (i, 0))],
        out_shape=[jax.ShapeDtypeStruct((t, d), BF16), jax.ShapeDtypeStruct((t, kd), F32)],
        compiler_params=_params(("parallel",)),
        name="norm_gate",
    )(x2, nw, wlr, upw, upb)


def _matmul_kernel(a_ref, b_ref, o_ref):
    o_ref[...] = _dot(a_ref[...], b_ref[...]).astype(o_ref.dtype)


def _in_proj(xn, w, tm, tn):
    t, d = xn.shape
    n = w.shape[1]
    return pl.pallas_call(
        _matmul_kernel,
        grid=(n // tn, t // tm),
        in_specs=[pl.BlockSpec((tm, d), lambda j, i: (i, 0)),
                  pl.BlockSpec((d, tn), lambda j, i: (0, j))],
        out_specs=pl.BlockSpec((tm, tn), lambda j, i: (i, j)),
        out_shape=jax.ShapeDtypeStruct((t, n), BF16),
        compiler_params=_params(("parallel", "parallel")),
        name="in_proj",
    )(xn, w)


def _decay_levels(w):
    c = w.shape[0]
    row = lax.broadcasted_iota(I32, (c, 1), 0)
    pre, suf, tot = w, None, w
    levels = []
    s = 1
    while s < 8:
        levels.append((pre, suf))
        odd = (row & s) != 0
        prev_tot = pltpu.roll(tot, s, 0)
        next_tot = pltpu.roll(tot, c - s, 0)
        pre = pre * jnp.where(odd, prev_tot, 1.0)
        grow = jnp.where(odd, 1.0, next_tot)
        suf = grow if suf is None else suf * grow
        tot = tot * jnp.where(odd, prev_tot, next_tot)
        s *= 2
    groups = c // 8
    pre_g = [pre[8 * r:8 * r + 8] for r in range(groups)]
    suf_g = [suf[8 * r:8 * r + 8] for r in range(groups)]
    tot_g = [tot[8 * r:8 * r + 8] for r in range(groups)]
    m = 1
    while 8 * m < c:
        levels.append((jnp.concatenate(pre_g, axis=0), jnp.concatenate(suf_g, axis=0)))
        new_tot = []
        for r in range(groups):
            if (r // m) & 1:
                pre_g[r] = pre_g[r] * tot_g[r - m]
                new_tot.append(tot_g[r] * tot_g[r - m])
            else:
                suf_g[r] = suf_g[r] * tot_g[r + m]
                new_tot.append(tot_g[r] * tot_g[r + m])
        tot_g = new_tot
        m *= 2
    return levels, jnp.concatenate(pre_g, axis=0), jnp.concatenate(suf_g, axis=0), tot_g[0][0:1]


def _score_masks(c):
    row = lax.broadcasted_iota(I32, (c, c), 0)
    col = lax.broadcasted_iota(I32, (c, c), 1)
    masks = [row == col]
    s, sh = 1, 0
    while s < c:
        masks.append((((row ^ col) >> sh) == 1) & ((row & s) != 0))
        s *= 2
        sh += 1
    return masks


def _recurrence_chunk(q, k, w, v, r, nw, st_ref, o_ref, rows, n_heads, dk, dv, masks):
    levels, pre_c, suf_c, dec = _decay_levels(w)
    kb = k.astype(BF16)
    ops = [(q.astype(BF16), kb)]
    for pre, suf in levels:
        ops.append(((q * pre).astype(BF16), kb if suf is None else (k * suf).astype(BF16)))
    qd = (q * pre_c).astype(BF16)
    kd = (k * suf_c).astype(BF16)
    for h in range(n_heads):
        sk = slice(h * dk, (h + 1) * dk)
        sv = slice(h * dv, (h + 1) * dv)
        a = None
        for (qs, ks), m in zip(ops, masks):
            part = jnp.where(m, _dot_nt(qs[:, sk], ks[:, sk]), 0.0)
            a = part if a is None else a + part
        st = st_ref[h]
        vh = v[:, sv]
        o = _dot_nt(qd[:, sk], st.astype(BF16)) + _dot(a.astype(BF16), vh)
        st_ref[h] = st * dec[:, sk] + _dot_tn(vh, kd[:, sk])
        y = o * lax.rsqrt(jnp.mean(o * o, axis=-1, keepdims=True) + EPS) * nw
        rh = r[:, sv]
        o_ref[rows, sv] = (y * (rh * jax.nn.sigmoid(rh))).astype(o_ref.dtype)


def _gla_kernel(q_ref, k_ref, v_ref, r_ref, g_ref, nw_ref, o_ref, st_ref, *, scale, n_chunks):
    @pl.when(pl.program_id(1) == 0)
    def _():
        st_ref[...] = jnp.zeros_like(st_ref)

    masks = _score_masks(CHUNK)
    n_heads, dv, dk = st_ref.shape
    nw = nw_ref[...]

    def body(ci, carry):
        rows = pl.ds(pl.multiple_of(ci * CHUNK, CHUNK), CHUNK)
        q = q_ref[rows, :].astype(F32) * scale
        k = k_ref[rows, :].astype(F32)
        _recurrence_chunk(q, k, jnp.exp(g_ref[rows, :]), v_ref[rows, :], r_ref[rows, :].astype(F32), nw,
                          st_ref, o_ref, rows, n_heads, dk, dv, masks)
        return carry

    lax.fori_loop(0, n_chunks, body, 0, unroll=2)


def _gla(proj, g, nw, batch, seq, ts):
    t, _ = proj.shape
    kd = g.shape[1]
    gw = 2 * kd
    hk = kd // GLA_HEADS
    hv = gw // GLA_HEADS
    ns = seq // ts
    row = lambda b, s: b * ns + s
    return pl.pallas_call(
        functools.partial(_gla_kernel, scale=hk ** -0.5, n_chunks=ts // CHUNK),
        grid=(batch, ns),
        in_specs=[pl.BlockSpec((ts, kd), lambda b, s: (row(b, s), 0)),
                  pl.BlockSpec((ts, kd), lambda b, s: (row(b, s), 1)),
                  pl.BlockSpec((ts, gw), lambda b, s: (row(b, s), 1)),
                  pl.BlockSpec((ts, gw), lambda b, s: (row(b, s), 2)),
                  pl.BlockSpec((ts, kd), lambda b, s: (row(b, s), 0)),
                  pl.BlockSpec((1, hv), lambda b, s: (0, 0))],
        out_specs=pl.BlockSpec((ts, gw), lambda b, s: (row(b, s), 0)),
        out_shape=jax.ShapeDtypeStruct((t, gw), BF16),
        scratch_shapes=[pltpu.VMEM((GLA_HEADS, hv, hk), F32)],
        compiler_params=_params(("parallel", "arbitrary")),
        name="gla",
    )(proj, proj, proj, proj, g, nw)


def _hgrn_kernel(q_ref, f_ref, i_ref, r_ref, lb_ref, nw_ref, o_ref, st_ref, *, n_chunks):
    @pl.when(pl.program_id(1) == 0)
    def _():
        st_ref[...] = jnp.zeros_like(st_ref)

    masks = _score_masks(CHUNK)
    n_heads = st_ref.shape[0]
    lb = lb_ref[...]
    nw = nw_ref[...]

    def body(ci, carry):
        rows = pl.ds(pl.multiple_of(ci * CHUNK, CHUNK), CHUNK)
        zf = f_ref[rows, :].astype(F32)
        forget = lb + (1.0 - lb) * jax.nn.sigmoid(zf)
        k = (1.0 - lb) * jax.nn.sigmoid(-zf)
        qz = q_ref[rows, :].astype(F32)
        q = qz * jax.nn.sigmoid(qz)
        _recurrence_chunk(q, k, forget, i_ref[rows, :], r_ref[rows, :].astype(F32), nw,
                          st_ref, o_ref, rows, n_heads, HGRN_EXPAND, HGRN_EXPAND, masks)
        return carry

    lax.fori_loop(0, n_chunks, body, 0, unroll=2)


def _hgrn(proj, lb, nw, batch, seq, ts, hw, col0):
    t, _ = proj.shape
    ns = seq // ts
    n_heads = hw // HGRN_EXPAND
    row = lambda b, s: b * ns + s
    return pl.pallas_call(
        functools.partial(_hgrn_kernel, n_chunks=ts // CHUNK),
        grid=(batch, ns),
        in_specs=[pl.BlockSpec((ts, hw), lambda b, s: (row(b, s), col0)),
                  pl.BlockSpec((ts, hw), lambda b, s: (row(b, s), col0 + 1)),
                  pl.BlockSpec((ts, hw), lambda b, s: (row(b, s), col0 + 2)),
                  pl.BlockSpec((ts, hw), lambda b, s: (row(b, s), col0 + 3)),
                  pl.BlockSpec((1, hw), lambda b, s: (0, 0)),
                  pl.BlockSpec((1, HGRN_EXPAND), lambda b, s: (0, 0))],
        out_specs=pl.BlockSpec((ts, hw), lambda b, s: (row(b, s), 0)),
        out_shape=jax.ShapeDtypeStruct((t, hw), BF16),
        scratch_shapes=[pltpu.VMEM((n_heads, HGRN_EXPAND, HGRN_EXPAND), F32)],
        compiler_params=_params(("parallel", "arbitrary")),
        name="hgrn",
    )(proj, proj, proj, proj, lb, nw)


def _out_proj_kernel(oa_ref, ob_ref, wa_ref, wb_ref, x_ref, h_ref):
    h_ref[...] = x_ref[...] + _dot(oa_ref[...], wa_ref[...]) + _dot(ob_ref[...], wb_ref[...])


def _out_proj(oa, ob, w, x2, tm, tn):
    t, d = x2.shape
    ka, kb = oa.shape[1], ob.shape[1]
    assert ka == kb and w.shape[0] == ka + kb
    return pl.pallas_call(
        _out_proj_kernel,
        grid=(d // tn, t // tm),
        in_specs=[pl.BlockSpec((tm, ka), lambda j, i: (i, 0)),
                  pl.BlockSpec((tm, kb), lambda j, i: (i, 0)),
                  pl.BlockSpec((ka, tn), lambda j, i: (0, j)),
                  pl.BlockSpec((kb, tn), lambda j, i: (1, j)),
                  pl.BlockSpec((tm, tn), lambda j, i: (i, j))],
        out_specs=pl.BlockSpec((tm, tn), lambda j, i: (i, j)),
        out_shape=jax.ShapeDtypeStruct((t, d), F32),
        compiler_params=_params(("parallel", "parallel")),
        name="out_proj",
    )(oa, ob, w, w, x2)


def _router_kernel(h_ref, nw_ref, rw_ref, rb_ref, topi_ref, topw_ref, rank_ref, cnt_ref, run_ref):
    step = pl.program_id(0)

    @pl.when(step == 0)
    def _():
        run_ref[...] = jnp.zeros_like(run_ref)

    tr = h_ref.shape[0]
    xn = _rmsnorm(h_ref[...], nw_ref[...])
    x_hi = xn.astype(BF16)
    x_lo = (xn - x_hi.astype(F32)).astype(BF16)
    rw = rw_ref[...]
    w_hi = rw.astype(BF16)
    w_lo = (rw - w_hi.astype(F32)).astype(BF16)
    logits = _dot(x_hi, w_hi) + (_dot(x_hi, w_lo) + _dot(x_lo, w_hi)) + rb_ref[...]
    lane = lax.broadcasted_iota(I32, (tr, LANES), 1)
    neg = jnp.float32(-jnp.inf)
    cur = jnp.where(lane < N_EXPERTS, logits, neg)
    vals, sels = [], []
    topi = jnp.zeros((tr, LANES), I32)
    for kk in range(TOP_K):
        m = jnp.max(cur, axis=-1, keepdims=True)
        idx = jnp.min(jnp.where(cur == m, lane, LANES), axis=-1, keepdims=True)
        sel = lane == idx
        vals.append(m)
        sels.append(sel)
        topi = jnp.where(lane == kk, idx, topi)
        cur = jnp.where(sel, neg, cur)
    exps = [jnp.exp(v - vals[0]) for v in vals]
    denom = exps[0] + exps[1] + exps[2] + exps[3]
    topw = jnp.zeros((tr, LANES), F32)
    hot = jnp.zeros((tr, LANES), F32)
    for kk in range(TOP_K):
        topw = jnp.where(lane == kk, exps[kk] / denom, topw)
        hot = hot + sels[kk].astype(F32)
    r_i = lax.broadcasted_iota(I32, (tr, tr), 0)
    c_i = lax.broadcasted_iota(I32, (tr, tr), 1)
    lower = (c_i < r_i).astype(BF16)
    before = _dot(lower, hot.astype(BF16)) + run_ref[...]
    rank = jnp.zeros((tr, LANES), F32)
    for kk in range(TOP_K):
        rk = jnp.sum(jnp.where(sels[kk], before, 0.0), axis=-1, keepdims=True)
        rank = jnp.where(lane == kk, rk, rank)
    topi_ref[...] = topi
    topw_ref[...] = topw
    rank_ref[...] = rank.astype(I32)
    run_ref[...] = run_ref[...] + jnp.sum(hot, axis=0, keepdims=True)
    cnt_ref[...] = run_ref[...]


def _router(h1, nw, rw, rb, tr):
    t, d = h1.shape
    tile = lambda i: (i, 0)
    fixed = lambda i: (0, 0)
    return pl.pallas_call(
        _router_kernel,
        grid=(t // tr,),
        in_specs=[pl.BlockSpec((tr, d), tile),
                  pl.BlockSpec((1, d), fixed),
                  pl.BlockSpec((d, LANES), fixed),
                  pl.BlockSpec((1, LANES), fixed)],
        out_specs=[pl.BlockSpec((tr, LANES), tile),
                   pl.BlockSpec((tr, LANES), tile),
                   pl.BlockSpec((tr, LANES), tile),
                   pl.BlockSpec((1, LANES), fixed)],
        out_shape=[jax.ShapeDtypeStruct((t, LANES), I32),
                   jax.ShapeDtypeStruct((t, LANES), F32),
                   jax.ShapeDtypeStruct((t, LANES), I32),
                   jax.ShapeDtypeStruct((1, LANES), F32)],
        scratch_shapes=[pltpu.VMEM((1, LANES), F32)],
        compiler_params=_params(("arbitrary",)),
        name="router",
    )(h1, nw, rw, rb)


def _zero_rows(pad_ref, xs_ref, zero_ref, sem, n_spans, wait):
    zr = zero_ref.shape[0]

    def run(copy):
        copy.wait() if wait else copy.start()

    def span(e, carry):
        start = pad_ref[0, 0, e]

        def one(j, c2):
            run(pltpu.make_async_copy(zero_ref.at[pl.ds(0, 1)], xs_ref.at[pl.ds(start + j, 1)], sem))
            return c2

        lax.fori_loop(0, pad_ref[0, 0, n_spans + e], one, 0)
        return carry

    lax.fori_loop(0, n_spans - 1, span, 0)
    tail = pad_ref[0, 0, n_spans - 1]

    def big(j, carry):
        run(pltpu.make_async_copy(zero_ref, xs_ref.at[pl.ds(pl.multiple_of(tail + j * zr, zr), zr)], sem))
        return carry

    lax.fori_loop(0, pad_ref[0, 0, 2 * n_spans - 1] // zr, big, 0)


def _dispatch_kernel(pos_ref, pad_ref, h_ref, nw_ref, xs_ref, buf_ref, zero_ref, sem, zsem, *, n_spans):
    i = pl.program_id(0)
    n = pl.num_programs(0)
    td = h_ref.shape[0]
    slot = i % 2

    def wait_rows(sl):
        for _ in range(TOP_K):
            pltpu.make_async_copy(buf_ref.at[sl], xs_ref.at[pl.ds(0, td)], sem.at[sl]).wait()

    @pl.when(i == 0)
    def _():
        zeros = jnp.zeros(zero_ref.shape, F32)
        zero_ref[...] = _pack2(zeros, zeros)
        _zero_rows(pad_ref, xs_ref, zero_ref, zsem, n_spans, wait=False)

    @pl.when(i >= 2)
    def _():
        wait_rows(slot)

    buf_ref[slot] = _pack_bf16_pairs(_rmsnorm(h_ref[...], nw_ref[...]))

    def issue(ti, carry):
        for kk in range(TOP_K):
            p = pos_ref[0, 0, kk * td + ti]
            pltpu.make_async_copy(buf_ref.at[slot, pl.ds(ti, 1)], xs_ref.at[pl.ds(p, 1)], sem.at[slot]).start()
        return carry

    lax.fori_loop(0, td, issue, 0)

    @pl.when(i == 0)
    def _():
        _zero_rows(pad_ref, xs_ref, zero_ref, zsem, n_spans, wait=True)

    @pl.when(i == n - 1)
    def _():
        wait_rows(slot)

    @pl.when((i == n - 1) & (n > 1))
    def _():
        wait_rows(1 - slot)


def _dispatch(pos_tiles, pad_spans, h1, nw, n_rows, td):
    t, d = h1.shape
    n_spans = pad_spans.shape[2] // 2
    return pl.pallas_call(
        functools.partial(_dispatch_kernel, n_spans=n_spans),
        grid=(t // td,),
        in_specs=[pl.BlockSpec((1, 1, TOP_K * td), lambda i: (i, 0, 0), memory_space=pltpu.SMEM),
                  pl.BlockSpec((1, 1, 2 * n_spans), lambda i: (0, 0, 0), memory_space=pltpu.SMEM),
                  pl.BlockSpec((td, d), lambda i: (i, 0)),
                  pl.BlockSpec((1, d), lambda i: (0, 0))],
        out_specs=pl.BlockSpec(memory_space=pl.ANY),
        out_shape=jax.ShapeDtypeStruct((n_rows, d // 2), U32),
        scratch_shapes=[pltpu.VMEM((2, td, d // 2), U32), pltpu.VMEM((td, d // 2), U32),
                        pltpu.SemaphoreType.DMA((2,)), pltpu.SemaphoreType.DMA],
        compiler_params=_params(("arbitrary",)),
        name="dispatch",
    )(pos_tiles, pad_spans, h1, nw)


def _experts_kernel(te_ref, tb_ref, nu_ref, x_ref, gw_ref, uw_ref, gb_ref, ub_ref,
                    dlo_ref, dhi_ref, blo_ref, bhi_ref, y_ref, xb_ref, hid_ref, *, nf):
    del te_ref, tb_ref
    i = pl.program_id(0)
    s = pl.program_id(1)
    half = x_ref.shape[1]
    tf = gw_ref.shape[2]
    active = i < nu_ref[0]

    @pl.when(active & (s == 0))
    def _():
        u = x_ref[...]
        xb_ref[:, :half] = _unpack_lo(u).astype(BF16)
        xb_ref[:, half:] = _unpack_hi(u).astype(BF16)

    @pl.when(active & (s < nf))
    def _():
        xb = xb_ref[...]
        g = jnp.minimum(_dot(xb, gw_ref[0]) + gb_ref[0], SWIGLU_LIMIT)
        u = jnp.clip(_dot(xb, uw_ref[0]) + ub_ref[0], -SWIGLU_LIMIT, SWIGLU_LIMIT)
        hid_ref[s] = (g * jax.nn.sigmoid(SWIGLU_ALPHA * g) * (u + 1.0)).astype(BF16)

    @pl.when(active & (s >= nf))
    def _():
        lo = blo_ref[0]
        hi = bhi_ref[0]
        for f in range(nf):
            hf = hid_ref[f]
            lo = lo + _dot(hf, dlo_ref[0, f * tf:(f + 1) * tf, :])
            hi = hi + _dot(hf, dhi_ref[0, f * tf:(f + 1) * tf, :])
        y_ref[...] = _pack2(lo, hi)

    @pl.when(jnp.logical_not(active) & (s >= nf))
    def _():
        zeros = jnp.zeros(y_ref.shape, F32)
        y_ref[...] = _pack2(zeros, zeros)


def _experts(tile_expert, tile_block, n_used, xs, gw, gb, uw, ub, dw, db, tm, tf, tn):
    p, half = xs.shape
    d = 2 * half
    ff = gw.shape[2]
    nf = ff // tf
    nn = half // tn
    n_tiles = p // tm

    def fidx(i, s, nu):
        return jnp.where(i < nu[0], jnp.minimum(s, nf - 1), nf - 1)

    def nidx(i, s, nu):
        return jnp.where(i < nu[0], jnp.maximum(s - nf, 0), nn - 1)

    def ahead(i, s):
        return jnp.where(s >= nf, jnp.minimum(i + 1, n_tiles - 1), i)

    grid_spec = pltpu.PrefetchScalarGridSpec(
        num_scalar_prefetch=3,
        grid=(n_tiles, nf + nn),
        in_specs=[pl.BlockSpec((tm, half), lambda i, s, te, tb, nu: (tb[ahead(i, s)], 0)),
                  pl.BlockSpec((1, d, tf), lambda i, s, te, tb, nu: (te[i], 0, fidx(i, s, nu))),
                  pl.BlockSpec((1, d, tf), lambda i, s, te, tb, nu: (te[ahead(i, s)], 0, fidx(ahead(i, s), jnp.where(s >= nf, 0, s), nu))),
                  pl.BlockSpec((1, 1, tf), lambda i, s, te, tb, nu: (te[i], 0, fidx(i, s, nu))),
                  pl.BlockSpec((1, 1, tf), lambda i, s, te, tb, nu: (te[i], 0, fidx(i, s, nu))),
                  pl.BlockSpec((1, ff, tn), lambda i, s, te, tb, nu: (te[i], 0, nidx(i, s, nu))),
                  pl.BlockSpec((1, ff, tn), lambda i, s, te, tb, nu: (te[i], 0, nn + nidx(i, s, nu))),
                  pl.BlockSpec((1, 1, tn), lambda i, s, te, tb, nu: (te[i], 0, nidx(i, s, nu))),
                  pl.BlockSpec((1, 1, tn), lambda i, s, te, tb, nu: (te[i], 0, nn + nidx(i, s, nu)))],
        out_specs=pl.BlockSpec((tm, tn), lambda i, s, te, tb, nu: (i, jnp.maximum(s - nf, 0))),
        scratch_shapes=[pltpu.VMEM((tm, d), BF16), pltpu.VMEM((nf, tm, tf), BF16)],
    )
    return pl.pallas_call(
        functools.partial(_experts_kernel, nf=nf),
        grid_spec=grid_spec,
        out_shape=jax.ShapeDtypeStruct((p, half), U32),
        compiler_params=_params(("arbitrary", "arbitrary")),
        name="experts",
    )(tile_expert, tile_block, n_used, xs, gw, uw, gb, ub, dw, dw, db, db)


def _combine_kernel(pos_ref, nxt_ref, h_ref, w_ref, nw_ref, y_ref, o_ref, buf_ref, sem):
    i = pl.program_id(0)
    n = pl.num_programs(0)
    tc = h_ref.shape[0]
    half = buf_ref.shape[3]
    slot = i % 2

    def gather(idx_ref, sl):
        def issue(ti, carry):
            for kk in range(TOP_K):
                p = idx_ref[0, 0, kk * tc + ti]
                pltpu.make_async_copy(y_ref.at[pl.ds(p, 1)], buf_ref.at[sl, kk, pl.ds(ti, 1)], sem.at[sl]).start()
            return carry

        lax.fori_loop(0, tc, issue, 0)

    @pl.when(i == 0)
    def _():
        gather(pos_ref, slot)

    @pl.when(i + 1 < n)
    def _():
        gather(nxt_ref, 1 - slot)

    for kk in range(TOP_K):
        pltpu.make_async_copy(y_ref.at[pl.ds(0, tc)], buf_ref.at[slot, kk], sem.at[slot]).wait()

    w = w_ref[...]
    h = h_ref[...]
    lo = h[:, :half]
    hi = h[:, half:]
    for kk in range(TOP_K):
        u = buf_ref[slot, kk]
        wk = w[:, kk:kk + 1]
        lo = lo + wk * _unpack_lo(u)
        hi = hi + wk * _unpack_hi(u)
    ms = (jnp.sum(lo * lo, axis=-1, keepdims=True) + jnp.sum(hi * hi, axis=-1, keepdims=True)) / (2 * half)
    inv = lax.rsqrt(ms + EPS)
    nw = nw_ref[...]
    o_ref[:, :half] = lo * inv * nw[:, :half]
    o_ref[:, half:] = hi * inv * nw[:, half:]


def _combine(pos_tiles, h1, topw, nw, y, tc):
    t, d = h1.shape
    n = t // tc
    return pl.pallas_call(
        _combine_kernel,
        grid=(n,),
        in_specs=[pl.BlockSpec((1, 1, TOP_K * tc), lambda i: (i, 0, 0), memory_space=pltpu.SMEM),
                  pl.BlockSpec((1, 1, TOP_K * tc), lambda i: (jnp.minimum(i + 1, n - 1), 0, 0),
                               memory_space=pltpu.SMEM),
                  pl.BlockSpec((tc, d), lambda i: (i, 0)),
                  pl.BlockSpec((tc, LANES), lambda i: (i, 0)),
                  pl.BlockSpec((1, d), lambda i: (0, 0)),
                  pl.BlockSpec(memory_space=pl.ANY)],
        out_specs=pl.BlockSpec((tc, d), lambda i: (i, 0)),
        out_shape=jax.ShapeDtypeStruct((t, d), F32),
        scratch_shapes=[pltpu.VMEM((2, TOP_K, tc, d // 2), U32), pltpu.SemaphoreType.DMA((2,))],
        compiler_params=_params(("arbitrary",)),
        name="combine",
    )(pos_tiles, pos_tiles, h1, topw, nw, y)


def _tiles(t, seq, d):
    return dict(
        norm_tm=min(512, t),
        proj_tm=min(1024, t), proj_tn=min(1024, d // 2),
        rec_ts=min(512, seq),
        out_tm=min(512, t), out_tn=min(1024, d),
        router_tr=min(512, t),
        scatter_td=min(256, t),
        expert_tm=min(512, t), expert_tf=min(512, d // 4), expert_tn=min(1024, d // 2),
        combine_tc=min(256, t),
    )


def _pos_tiles(pos, tile):
    t = pos.shape[0]
    return pos.reshape(t // tile, tile, TOP_K).transpose(0, 2, 1).reshape(t // tile, 1, TOP_K * tile)


def kernel(x, attn_norm_w, in_proj_w, gla_gate_up_w, gla_gate_up_b, gla_out_norm_w, hgrn_lb_logits, hgrn_out_norm_w, out_proj_w, ffn_norm_w, router_w, router_b, expert_gate_w, expert_gate_b, expert_up_w, expert_up_b, expert_down_w, expert_down_b, final_norm_w):
    batch, seq, d = x.shape
    depth = attn_norm_w.shape[0]
    assert depth == 1 and d % 2048 == 0 and seq % CHUNK == 0
    t = batch * seq
    kd = d // 4
    gw = d // 2
    hw = d - gw
    ff = expert_gate_w.shape[-1]
    ts = _tiles(t, seq, d)

    lower = jnp.cumsum(jax.nn.softmax(hgrn_lb_logits.astype(F32), axis=0), axis=0)[0:1]

    w_in = in_proj_w[0]
    c_lr = 2 * kd + 2 * gw
    w_main = jnp.concatenate([w_in[:, :c_lr].astype(BF16), w_in[:, c_lr + GLA_RANK:].astype(BF16)], axis=1)
    w_lr = jnp.pad(w_in[:, c_lr:c_lr + GLA_RANK], ((0, 0), (0, LANES - GLA_RANK))).astype(BF16)
    up_w = jnp.pad(gla_gate_up_w[0], ((0, LANES - GLA_RANK), (0, 0)))
    w_out = out_proj_w[0].astype(BF16)
    r_w = jnp.pad(router_w[0], ((0, 0), (0, LANES - N_EXPERTS)))
    r_b = jnp.pad(router_b, ((0, 0), (0, LANES - N_EXPERTS)))

    x2 = x.reshape(t, d)
    xn, g_gla = _norm_gate(x2, attn_norm_w, w_lr, up_w, gla_gate_up_b, ts["norm_tm"])
    proj = _in_proj(xn, w_main, ts["proj_tm"], ts["proj_tn"])
    o_a = _gla(proj, g_gla, gla_out_norm_w, batch, seq, ts["rec_ts"])
    o_b = _hgrn(proj, lower, hgrn_out_norm_w, batch, seq, ts["rec_ts"], hw, c_lr // hw)
    h1 = _out_proj(o_a, o_b, w_out, x2, ts["out_tm"], ts["out_tn"])

    topi, topw, rank, counts = _router(h1, ffn_norm_w, r_w, r_b, ts["router_tr"])

    tm = ts["expert_tm"]
    assert tm % ts["scatter_td"] == 0
    n_tiles = (t * TOP_K) // tm + N_EXPERTS
    cnt = counts[0, :N_EXPERTS].astype(I32)
    tiles_e = (cnt + tm - 1) // tm
    tile_end = jnp.cumsum(tiles_e)
    row_start = (tile_end - tiles_e) * tm
    n_used = tile_end[-1]
    pos = row_start[topi[:, :TOP_K]] + rank[:, :TOP_K]
    tile_id = jnp.arange(n_tiles, dtype=I32)
    tile_block = jnp.minimum(tile_id, n_used - 1)
    tile_expert = jnp.sum((tile_block[:, None] >= tile_end[None, :]).astype(I32), axis=1)

    pad_start = jnp.concatenate([row_start + cnt, (n_used * tm)[None]])
    pad_len = jnp.concatenate([tiles_e * tm - cnt, ((n_tiles - n_used) * tm)[None]])
    pad_spans = jnp.concatenate([pad_start, pad_len]).astype(I32)[None, None, :]
    xs = _dispatch(_pos_tiles(pos, ts["scatter_td"]), pad_spans, h1, ffn_norm_w, n_tiles * tm, ts["scatter_td"])
    y = _experts(tile_expert, tile_block, n_used.reshape(1), xs,
                 expert_gate_w[0].astype(BF16), expert_gate_b[0][:, None, :],
                 expert_up_w[0].astype(BF16), expert_up_b[0][:, None, :],
                 expert_down_w[0].astype(BF16), expert_down_b[0][:, None, :], tm, ts["expert_tf"], ts["expert_tn"])
    out = _combine(_pos_tiles(pos, ts["combine_tc"]), h1, topw, final_norm_w[None, :], y, ts["combine_tc"])
    return out.reshape(batch, seq, d)
```
